```python
import jax, jax.numpy as jnp
from jax import lax
import numpy as np

D_MODEL = 1024
BATCH = 2
SEQ = 8192
DEPTH = 2

N_META = 16
D_MIX = D_MODEL
ATTN_HEADS = 8
HEAD_DIM = 64
D_ATTN = ATTN_HEADS * HEAD_DIM
D_CONF = D_MIX // 4
D_SC = D_MIX - D_ATTN - D_CONF
CONF_KERNEL = 31
SC_KERNEL = 3
D_FF = 4 * D_MODEL
Q_BLOCK = 128
EPS = 1e-6
N_IN = 3 * D_ATTN + ATTN_HEADS + 2 * D_CONF + 3 * D_SC

kernel_name = 'hybrid_fox_conformer_shortconv_block'


def _rmsnorm(x, g):
    xf = x.astype(jnp.float32)
    y = xf * lax.rsqrt(jnp.mean(xf * xf, axis=-1, keepdims=True) + EPS)
    return (y * g.astype(jnp.float32)).astype(x.dtype)


def _layernorm(x, g, b):
    xf = x.astype(jnp.float32)
    mu = jnp.mean(xf, axis=-1, keepdims=True)
    xc = xf - mu
    y = xc * lax.rsqrt(jnp.mean(xc * xc, axis=-1, keepdims=True) + EPS)
    return (y * g.astype(jnp.float32) + b.astype(jnp.float32)).astype(x.dtype)


def _causal_dwconv(x, w):
    K, C = w.shape
    return lax.conv_general_dilated(
        x, w[:, None, :].astype(x.dtype), window_strides=(1,), padding=[(K - 1, 0)],
        dimension_numbers=('NWC', 'WIO', 'NWC'), feature_group_count=C)


def _fox_attend(q_blk, c_q, q_pos, k, v, c_k, k_pos):
    s = jnp.einsum('bhqd,bhkd->bhqk', q_blk, k, preferred_element_type=jnp.float32) * (HEAD_DIM ** -0.5)
    s = s + c_q[..., :, None] - c_k[..., None, :]
    s = jnp.where(q_pos[:, None] >= k_pos[None, :], s, -jnp.inf)
    p = jax.nn.softmax(s, axis=-1)
    return jnp.einsum('bhqk,bhkd->bhqd', p.astype(v.dtype), v)


def _fox_attention(q, k, v, log_f):
    bsz, L, H, dh = q.shape
    n_real = L - N_META
    nb = n_real // Q_BLOCK
    c = jnp.cumsum(log_f, axis=1).transpose(0, 2, 1)
    qh = q.transpose(0, 2, 1, 3)
    kh = k.transpose(0, 2, 1, 3)
    vh = v.transpose(0, 2, 1, 3)
    k_pos = jnp.arange(L)
    meta_out = _fox_attend(qh[:, :, :N_META], c[:, :, :N_META], k_pos[:N_META],
                           kh[:, :, :N_META], vh[:, :, :N_META], c[:, :, :N_META], k_pos[:N_META])
    q_blocks = qh[:, :, N_META:].reshape(bsz, H, nb, Q_BLOCK, dh).transpose(2, 0, 1, 3, 4)
    c_blocks = c[:, :, N_META:].reshape(bsz, H, nb, Q_BLOCK).transpose(2, 0, 1, 3)

    def body(args):
        q_blk, c_q, i = args
        q_pos = N_META + i * Q_BLOCK + jnp.arange(Q_BLOCK)
        return _fox_attend(q_blk, c_q, q_pos, kh, vh, c, k_pos)

    real_out = lax.map(body, (q_blocks, c_blocks, jnp.arange(nb)))
    real_out = real_out.transpose(1, 0, 3, 2, 4).reshape(bsz, n_real, H * dh)
    meta_out = meta_out.transpose(0, 2, 1, 3).reshape(bsz, N_META, H * dh)
    return jnp.concatenate([meta_out, real_out], axis=1)


def setup_inputs(seed: int = 0) -> dict:
    key = jax.random.key(seed)
    ks = jax.random.split(key, 20)
    nrm = jax.random.normal
    f32 = jnp.float32
    return {
        'x': nrm(ks[0], (BATCH, SEQ, D_MODEL), f32),
        'meta_tokens': nrm(ks[1], (N_META, D_MODEL), f32),
        'mix_norm_g': 1.0 + 0.1 * nrm(ks[2], (DEPTH, D_MODEL), f32),
        'w_in': nrm(ks[3], (DEPTH, D_MODEL, N_IN), f32) * D_MODEL ** -0.5,
        'b_forget': jax.random.uniform(ks[4], (DEPTH, ATTN_HEADS), f32, 1.0, 5.0),
        'w_conf_dw': nrm(ks[5], (DEPTH, CONF_KERNEL, D_CONF), f32) * CONF_KERNEL ** -0.5,
        'b_conf_dw': 0.02 * nrm(ks[6], (DEPTH, D_CONF), f32),
        'conf_ln_g': 1.0 + 0.1 * nrm(ks[7], (DEPTH, D_CONF), f32),
        'conf_ln_b': 0.02 * nrm(ks[8], (DEPTH, D_CONF), f32),
        'w_conf_pw': nrm(ks[9], (DEPTH, D_CONF, D_CONF), f32) * D_CONF ** -0.5,
        'b_conf_pw': 0.02 * nrm(ks[10], (DEPTH, D_CONF), f32),
        'w_sc_conv': nrm(ks[11], (DEPTH, SC_KERNEL, D_SC), f32) * SC_KERNEL ** -0.5,
        'w_out': nrm(ks[12], (DEPTH, D_MIX, D_MODEL), f32) * D_MIX ** -0.5,
        'mlp_norm_g': 1.0 + 0.1 * nrm(ks[13], (DEPTH, D_MODEL), f32),
        'w_mlp1': nrm(ks[14], (DEPTH, D_MODEL, D_FF), f32) * D_MODEL ** -0.5,
        'w_mlp2': nrm(ks[15], (DEPTH, D_FF, D_MODEL), f32) * D_FF ** -0.5,
        'final_norm_g': 1.0 + 0.1 * nrm(ks[16], (D_MODEL,), f32),
    }


def reference(x, meta_tokens, mix_norm_g, w_in, b_forget, w_conf_dw, b_conf_dw, conf_ln_g,
              conf_ln_b, w_conf_pw, b_conf_pw, w_sc_conv, w_out, mlp_norm_g, w_mlp1, w_mlp2,
              final_norm_g):
    bsz = x.shape[0]
    meta = jnp.broadcast_to(meta_tokens[None].astype(x.dtype), (bsz, N_META, D_MODEL))
    h = jnp.concatenate([meta, x], axis=1)
    L = h.shape[1]
    sizes = [D_ATTN, D_ATTN, D_ATTN, ATTN_HEADS, D_CONF, D_CONF, D_SC, D_SC, D_SC]
    splits = np.cumsum(sizes)[:-1].tolist()
    for l in range(DEPTH):
        hn = _rmsnorm(h, mix_norm_g[l])
        proj = hn @ w_in[l]
        q, k, v, f_logit, conf_a, conf_gate, sc_b, sc_c, sc_u = jnp.split(proj, splits, axis=-1)
        log_f = jax.nn.log_sigmoid(f_logit.astype(jnp.float32) + b_forget[l].astype(jnp.float32))
        attn = _fox_attention(q.reshape(bsz, L, ATTN_HEADS, HEAD_DIM),
                              k.reshape(bsz, L, ATTN_HEADS, HEAD_DIM),
                              v.reshape(bsz, L, ATTN_HEADS, HEAD_DIM), log_f)
        glu = conf_a * jax.nn.sigmoid(conf_gate)
        dw = _causal_dwconv(glu, w_conf_dw[l]) + b_conf_dw[l]
        conf = jax.nn.silu(_layernorm(dw, conf_ln_g[l], conf_ln_b[l])) @ w_conf_pw[l] + b_conf_pw[l]
        sc = sc_b * _causal_dwconv(sc_c * sc_u, w_sc_conv[l])
        h = h + jnp.concatenate([attn, conf, sc], axis=-1) @ w_out[l]
        hn = _rmsnorm(h, mlp_norm_g[l])
        h = h + jnp.square(jax.nn.relu(hn @ w_mlp1[l])) @ w_mlp2[l]
    out = _rmsnorm(h, final_norm_g)
    return out[:, N_META:]
```

```python
import functools

import jax
import jax.numpy as jnp
import numpy as np
from jax import lax
from jax.experimental import pallas as pl
from jax.experimental.pallas import tpu as pltpu

D_MODEL = 1024
N_META = 16
HEADS = 8
HEAD_DIM = 64
D_ATTN = HEADS * HEAD_DIM
D_CONF = 256
D_SC = 256
CONF_K = 31
SC_K = 3
D_FF = 4 * D_MODEL
EPS = 1e-6

LANES = 128
ROW_TILE = 512
BQ = 512
BK = 512
HALO = 32
FF_CHUNK = 1024
N_SPLIT = 3
VMEM_LIMIT = 56 * 1024 * 1024

C_F = 0
C_QK = C_F + LANES
C_V = C_QK + 2 * HEADS * LANES
C_REST = C_V + D_ATTN
N_PACK = C_REST + 2 * D_CONF + 3 * D_SC


def _split_bf16(x):
    pieces = []
    r = x
    for _ in range(N_SPLIT):
        p = r.astype(jnp.bfloat16)
        pieces.append(p)
        r = r - p.astype(jnp.float32)
    return pieces


def _rms(x, g):
    return x * lax.rsqrt(jnp.mean(x * x, axis=-1, keepdims=True) + EPS) * g


def _proj_kernel(h_ref, g_ref, w_ref, bf_ref, tri_ref, e_ref, qkb_ref,
                 qp_ref, kp_ref, vt_ref, mix_ref, carry_ref):
    @pl.when(pl.program_id(1) == 0)
    def _():
        carry_ref[...] = jnp.zeros_like(carry_ref)

    hn = _rms(h_ref[0], g_ref[...]).astype(jnp.bfloat16)

    z = jnp.dot(hn, w_ref[:, C_F:C_F + LANES], preferred_element_type=jnp.float32) + bf_ref[...]
    log_f = jnp.minimum(z, 0.0) - jnp.log1p(jnp.exp(-jnp.abs(z)))
    tri = tri_ref[...]
    c = carry_ref[0:1, :]
    for piece in _split_bf16(log_f):
        c = c + jnp.dot(tri, piece, preferred_element_type=jnp.float32)
    carry_ref[0:1, :] = c[ROW_TILE - 1:ROW_TILE, :]

    c_pieces = jnp.concatenate(_split_bf16(c), axis=-1)
    qk = (jnp.dot(hn, w_ref[:, C_QK:C_V], preferred_element_type=jnp.float32)
          + jnp.dot(c_pieces, e_ref[...], preferred_element_type=jnp.float32)
          + qkb_ref[...]).astype(jnp.bfloat16)
    for hd in range(HEADS):
        qp_ref[0, hd] = qk[:, hd * LANES:(hd + 1) * LANES]
        kp_ref[0, hd] = qk[:, (HEADS + hd) * LANES:(HEADS + hd + 1) * LANES]

    v = jnp.dot(hn, w_ref[:, C_V:C_REST], preferred_element_type=jnp.float32)
    vt = v.T.astype(jnp.bfloat16)
    for hd in range(HEADS):
        vt_ref[0, hd] = vt[hd * HEAD_DIM:(hd + 1) * HEAD_DIM, :]

    r = jnp.dot(hn, w_ref[:, C_REST:N_PACK], preferred_element_type=jnp.float32)
    conf_a = r[:, 0:D_CONF]
    conf_g = r[:, D_CONF:2 * D_CONF]
    sc_b = r[:, 2 * D_CONF:2 * D_CONF + D_SC]
    sc_c = r[:, 2 * D_CONF + D_SC:2 * D_CONF + 2 * D_SC]
    sc_u = r[:, 2 * D_CONF + 2 * D_SC:]
    mix_ref[0, :, 0:D_CONF] = conf_a * jax.nn.sigmoid(conf_g)
    mix_ref[0, :, D_CONF:D_CONF + D_SC] = sc_b
    mix_ref[0, :, D_CONF + D_SC:] = sc_c * sc_u


def _proj_call(h, g, w_pack, bf_row, tri, e_mat, qk_bias):
    bsz, l_pad, _ = h.shape
    nt = l_pad // ROW_TILE
    const = lambda b, i: (0, 0)
    return pl.pallas_call(
        _proj_kernel,
        grid=(bsz, nt),
        in_specs=[
            pl.BlockSpec((1, ROW_TILE, D_MODEL), lambda b, i: (b, i, 0)),
            pl.BlockSpec((1, D_MODEL), const),
            pl.BlockSpec((D_MODEL, N_PACK), const),
            pl.BlockSpec((1, LANES), const),
            pl.BlockSpec((ROW_TILE, ROW_TILE), const),
            pl.BlockSpec((N_SPLIT * LANES, 2 * HEADS * LANES), const),
            pl.BlockSpec((1, 2 * HEADS * LANES), const),
        ],
        out_specs=[
            pl.BlockSpec((1, HEADS, ROW_TILE, LANES), lambda b, i: (b, 0, i, 0)),
            pl.BlockSpec((1, HEADS, ROW_TILE, LANES), lambda b, i: (b, 0, i, 0)),
            pl.BlockSpec((1, HEADS, HEAD_DIM, ROW_TILE), lambda b, i: (b, 0, 0, i)),
            pl.BlockSpec((1, ROW_TILE, D_CONF + 2 * D_SC), lambda b, i: (b, i, 0)),
        ],
        out_shape=[
            jax.ShapeDtypeStruct((bsz, HEADS, l_pad, LANES), jnp.bfloat16),
            jax.ShapeDtypeStruct((bsz, HEADS, l_pad, LANES), jnp.bfloat16),
            jax.ShapeDtypeStruct((bsz, HEADS, HEAD_DIM, l_pad), jnp.bfloat16),
            jax.ShapeDtypeStruct((bsz, l_pad, D_CONF + 2 * D_SC), jnp.float32),
        ],
        scratch_shapes=[pltpu.VMEM((8, LANES), jnp.float32)],
        compiler_params=pltpu.CompilerParams(
            dimension_semantics=("arbitrary", "arbitrary"), vmem_limit_bytes=VMEM_LIMIT),
        name="proj",
    )(h, g, w_pack, bf_row, tri, e_mat, qk_bias)


def _attn_kernel(qp_ref, kp_ref, vt_ref, o_ref):
    qi = pl.program_id(2)
    q = qp_ref[0, 0]

    def block(j, carry, masked):
        m, l, acc = carry
        k0 = pl.multiple_of(j * BK, BK)
        kblk = kp_ref[0, 0, pl.ds(k0, BK), :]
        s = lax.dot_general(kblk, q, (((1,), (1,)), ((), ())),
                            preferred_element_type=jnp.float32)
        if masked:
            kpos = lax.broadcasted_iota(jnp.int32, (BK, BQ), 0)
            qpos = lax.broadcasted_iota(jnp.int32, (BK, BQ), 1)
            s = jnp.where(kpos <= qpos, s, -jnp.inf)
        m_new = jnp.maximum(m, jnp.max(s, axis=0, keepdims=True))
        alpha = jnp.exp(m - m_new)
        p = jnp.exp(s - m_new)
        l = alpha * l + jnp.sum(p, axis=0, keepdims=True)
        vblk = vt_ref[0, 0, :, pl.ds(k0, BK)]
        acc = alpha * acc + jnp.dot(vblk, p.astype(jnp.bfloat16),
                                    preferred_element_type=jnp.float32)
        return m_new, l, acc

    init = (jnp.full((1, BQ), -jnp.inf, jnp.float32),
            jnp.zeros((1, BQ), jnp.float32),
            jnp.zeros((HEAD_DIM, BQ), jnp.float32))
    carry = lax.fori_loop(0, qi, lambda j, cr: block(j, cr, False), init)
    m, l, acc = block(qi, carry, True)
    o_ref[0, 0] = (acc / l).astype(o_ref.dtype)


def _attn_call(qp, kp, vt):
    bsz, heads, l_pad, _ = qp.shape
    nq = l_pad // BQ
    return pl.pallas_call(
        _attn_kernel,
        grid=(bsz, heads, nq),
        in_specs=[
            pl.BlockSpec((1, 1, BQ, LANES), lambda b, h, i: (b, h, i, 0)),
            pl.BlockSpec((1, 1, l_pad, LANES), lambda b, h, i: (b, h, 0, 0)),
            pl.BlockSpec((1, 1, HEAD_DIM, l_pad), lambda b, h, i: (b, h, 0, 0)),
        ],
        out_specs=pl.BlockSpec((1, 1, HEAD_DIM, BQ), lambda b, h, i: (b, h, 0, i)),
        out_shape=jax.ShapeDtypeStruct((bsz, heads, HEAD_DIM, l_pad), jnp.bfloat16),
        compiler_params=pltpu.CompilerParams(
            dimension_semantics=("arbitrary", "arbitrary", "arbitrary"), vmem_limit_bytes=VMEM_LIMIT),
        name="attn",
    )(qp, kp, vt)


def _causal_taps(ext_ref, w_ref, n_taps):
    out = None
    for k in range(n_taps):
        term = ext_ref[pl.ds(HALO - (n_taps - 1) + k, ROW_TILE), :] * w_ref[k:k + 1, :]
        out = term if out is None else out + term
    return out


def _mix_kernel(h_ref, ot_ref, mix_ref, halo_ref, wdw_ref, bdw_ref, lng_ref, lnb_ref, wpw_ref, bpw_ref,
                wsc_ref, wout_ref, ng_ref, w1_ref, w2_ref, fg_ref, out_ref, ext_a, ext_c, *, final):
    first = pl.program_id(1) == 0
    halo = jnp.where(first, 0.0, halo_ref[0])

    ext_a[0:HALO, :] = halo[:, 0:D_CONF]
    ext_a[HALO:, :] = mix_ref[0, :, 0:D_CONF]
    dw = _causal_taps(ext_a, wdw_ref, CONF_K) + bdw_ref[...]
    mu = jnp.mean(dw, axis=-1, keepdims=True)
    xc = dw - mu
    y = xc * lax.rsqrt(jnp.mean(xc * xc, axis=-1, keepdims=True) + EPS) * lng_ref[...] + lnb_ref[...]
    y = y * jax.nn.sigmoid(y)
    conf = jnp.dot(y.astype(jnp.bfloat16), wpw_ref[...], preferred_element_type=jnp.float32) + bpw_ref[...]

    ext_c[0:HALO, :] = halo[:, D_CONF + D_SC:]
    ext_c[HALO:, :] = mix_ref[0, :, D_CONF + D_SC:]
    sc = mix_ref[0, :, D_CONF:D_CONF + D_SC] * _causal_taps(ext_c, wsc_ref, SC_K)

    attn = ot_ref[0].T
    mixed = (jnp.dot(attn, wout_ref[0:D_ATTN, :], preferred_element_type=jnp.float32)
             + jnp.dot(conf.astype(jnp.bfloat16), wout_ref[D_ATTN:D_ATTN + D_CONF, :],
                       preferred_element_type=jnp.float32)
             + jnp.dot(sc.astype(jnp.bfloat16), wout_ref[D_ATTN + D_CONF:, :],
                       preferred_element_type=jnp.float32))
    h1 = h_ref[0] + mixed

    hn = _rms(h1, ng_ref[...]).astype(jnp.bfloat16)
    acc = h1
    for c0 in range(0, D_FF, FF_CHUNK):
        a = jnp.dot(hn, w1_ref[:, c0:c0 + FF_CHUNK], preferred_element_type=jnp.float32)
        a = jnp.maximum(a, 0.0)
        acc = acc + jnp.dot((a * a).astype(jnp.bfloat16), w2_ref[c0:c0 + FF_CHUNK, :],
                            preferred_element_type=jnp.float32)
    if final:
        acc = _rms(acc, fg_ref[...])
    out_ref[0] = acc


def _mix_call(h, ot, mixin, wdw, bdw, lng, lnb, wpw, bpw, wsc, wout, ng, w1, w2, fg, *, final):
    bsz, l_pad, _ = h.shape
    nt = l_pad // ROW_TILE
    const = lambda b, i: (0, 0)
    once = pl.Buffered(1)
    wide = D_CONF + 2 * D_SC
    return pl.pallas_call(
        functools.partial(_mix_kernel, final=final),
        grid=(bsz, nt),
        in_specs=[
            pl.BlockSpec((1, ROW_TILE, D_MODEL), lambda b, i: (b, i, 0)),
            pl.BlockSpec((1, D_ATTN, ROW_TILE), lambda b, i: (b, 0, i)),
            pl.BlockSpec((1, ROW_TILE, wide), lambda b, i: (b, i, 0)),
            pl.BlockSpec((1, HALO, wide),
                         lambda b, i: (b, jnp.maximum(i * (ROW_TILE // HALO) - 1, 0), 0)),
            pl.BlockSpec((HALO, D_CONF), const),
            pl.BlockSpec((1, D_CONF), const),
            pl.BlockSpec((1, D_CONF), const),
            pl.BlockSpec((1, D_CONF), const),
            pl.BlockSpec((D_CONF, D_CONF), const),
            pl.BlockSpec((1, D_CONF), const),
            pl.BlockSpec((8, D_SC), const),
            pl.BlockSpec((D_MODEL, D_MODEL), const, pipeline_mode=once),
            pl.BlockSpec((1, D_MODEL), const),
            pl.BlockSpec((D_MODEL, D_FF), const, pipeline_mode=once),
            pl.BlockSpec((D_FF, D_MODEL), const, pipeline_mode=once),
            pl.BlockSpec((1, D_MODEL), const),
        ],
        out_specs=pl.BlockSpec((1, ROW_TILE, D_MODEL), lambda b, i: (b, i, 0)),
        out_shape=jax.ShapeDtypeStruct((bsz, l_pad, D_MODEL), jnp.float32),
        scratch_shapes=[pltpu.VMEM((HALO + ROW_TILE, D_CONF), jnp.float32),
                        pltpu.VMEM((HALO + ROW_TILE, D_SC), jnp.float32)],
        compiler_params=pltpu.CompilerParams(
            dimension_semantics=("arbitrary", "arbitrary"), vmem_limit_bytes=VMEM_LIMIT),
        name="mix_final" if final else "mix",
    )(h, ot, mixin, mixin, wdw, bdw, lng, lnb, wpw, bpw, wsc, wout, ng, w1, w2, fg)


def _decay_selectors():
    e = np.zeros((N_SPLIT * LANES, 2 * HEADS * LANES), np.float32)
    bias = np.zeros((1, 2 * HEADS * LANES), np.float32)
    for hd in range(HEADS):
        for j in range(N_SPLIT):
            e[j * LANES + hd, hd * LANES + HEAD_DIM + j] = 1.0
            bias[0, hd * LANES + HEAD_DIM + N_SPLIT + j] = 1.0
            bias[0, (HEADS + hd) * LANES + HEAD_DIM + j] = 1.0
            e[j * LANES + hd, (HEADS + hd) * LANES + HEAD_DIM + N_SPLIT + j] = -1.0
    return jnp.asarray(e, jnp.bfloat16), jnp.asarray(bias, jnp.float32)


def _pack_w_in(w):
    d = w.shape[0]
    q = (w[:, 0:D_ATTN] * (HEAD_DIM ** -0.5)).reshape(d, HEADS, HEAD_DIM)
    k = w[:, D_ATTN:2 * D_ATTN].reshape(d, HEADS, HEAD_DIM)
    pad = jnp.zeros((d, HEADS, LANES - HEAD_DIM), w.dtype)
    qk = jnp.concatenate([jnp.concatenate([q, pad], -1).reshape(d, HEADS * LANES),
                          jnp.concatenate([k, pad], -1).reshape(d, HEADS * LANES)], -1)
    v = w[:, 2 * D_ATTN:3 * D_ATTN]
    f = jnp.pad(w[:, 3 * D_ATTN:3 * D_ATTN + HEADS], ((0, 0), (0, LANES - HEADS)))
    rest = w[:, 3 * D_ATTN + HEADS:]
    return jnp.concatenate([f, qk, v, rest], -1).astype(jnp.bfloat16)


def kernel(x, meta_tokens, mix_norm_g, w_in, b_forget, w_conf_dw, b_conf_dw, conf_ln_g, conf_ln_b,
           w_conf_pw, b_conf_pw, w_sc_conv, w_out, mlp_norm_g, w_mlp1, w_mlp2, final_norm_g):
    bsz, seq, d = x.shape
    depth = w_in.shape[0]
    l_real = N_META + seq
    l_pad = -(-l_real // ROW_TILE) * ROW_TILE
    assert l_pad % BQ == 0 and l_pad % BK == 0 and BQ == BK

    meta = jnp.broadcast_to(meta_tokens[None].astype(x.dtype), (bsz, N_META, d))
    h = jnp.concatenate([meta, x, jnp.zeros((bsz, l_pad - l_real, d), x.dtype)], axis=1)

    tri = jnp.asarray(np.tril(np.ones((ROW_TILE, ROW_TILE), np.float32)), jnp.bfloat16)
    e_mat, qk_bias = _decay_selectors()
    row = lambda a: a.reshape(1, -1).astype(jnp.float32)
    fg = row(final_norm_g)

    for l in range(depth):
        bf_row = jnp.pad(row(b_forget[l]), ((0, 0), (0, LANES - HEADS)))
        qp, kp, vt, mixin = _proj_call(h, row(mix_norm_g[l]), _pack_w_in(w_in[l]), bf_row, tri, e_mat, qk_bias)
        ot = _attn_call(qp, kp, vt).reshape(bsz, D_ATTN, l_pad)
        wdw = jnp.pad(w_conf_dw[l], ((0, HALO - CONF_K), (0, 0)))
        wsc = jnp.pad(w_sc_conv[l], ((0, 8 - SC_K), (0, 0)))
        h = _mix_call(h, ot, mixin, wdw, row(b_conf_dw[l]), row(conf_ln_g[l]), row(conf_ln_b[l]),
                      w_conf_pw[l].astype(jnp.bfloat16), row(b_conf_pw[l]), wsc,
                      w_out[l].astype(jnp.bfloat16), row(mlp_norm_g[l]),
                      w_mlp1[l].astype(jnp.bfloat16), w_mlp2[l].astype(jnp.bfloat16), fg,
                      final=(l == depth - 1))
    return h[:, N_META:l_real]
```

```python
import functools

import jax
import jax.numpy as jnp
import numpy as np
from jax import lax
from jax.experimental import pallas as pl
from jax.experimental.pallas import tpu as pltpu

D_MODEL = 1024
N_META = 16
HEADS = 8
HEAD_DIM = 64
D_ATTN = HEADS * HEAD_DIM
D_CONF = 256
D_SC = 256
CONF_K = 31
SC_K = 3
D_FF = 4 * D_MODEL
EPS = 1e-6

LANES = 128
ROW_TILE = 512
BQ = 512
BK = 256
DIAG_BLOCKS = BQ // BK
HEADS_PER_STEP = 2
LOG2E = 1.4426950408889634
HALO = 32
FF_CHUNK = 1024
N_SPLIT = 3
VMEM_LIMIT = 56 * 1024 * 1024

C_F = 0
C_QK = C_F + LANES
C_V = C_QK + 2 * HEADS * LANES
C_REST = C_V + D_ATTN
N_PACK = C_REST + 2 * D_CONF + 3 * D_SC


def _split_bf16(x):
    pieces = []
    r = x
    for _ in range(N_SPLIT):
        p = r.astype(jnp.bfloat16)
        pieces.append(p)
        r = r - p.astype(jnp.float32)
    return pieces


def _rms(x, g):
    return x * lax.rsqrt(jnp.mean(x * x, axis=-1, keepdims=True) + EPS) * g


def _proj_kernel(h_ref, g_ref, w_ref, bf_ref, tri_ref, e_ref, qkb_ref,
                 qp_ref, kp_ref, vt_ref, mix_ref, carry_ref):
    @pl.when(pl.program_id(1) == 0)
    def _():
        carry_ref[...] = jnp.zeros_like(carry_ref)

    hn = _rms(h_ref[0], g_ref[...]).astype(jnp.bfloat16)

    z = jnp.dot(hn, w_ref[:, C_F:C_F + LANES], preferred_element_type=jnp.float32) + bf_ref[...]
    log_f = jnp.minimum(z, 0.0) - jnp.log1p(jnp.exp(-jnp.abs(z)))
    tri = tri_ref[...]
    c = carry_ref[0:1, :]
    for piece in _split_bf16(log_f):
        c = c + jnp.dot(tri, piece, preferred_element_type=jnp.float32)
    carry_ref[0:1, :] = c[ROW_TILE - 1:ROW_TILE, :]

    c_pieces = jnp.concatenate(_split_bf16(c * LOG2E), axis=-1)
    qk = jnp.dot(hn, w_ref[:, C_QK:C_V], preferred_element_type=jnp.float32)
    qk = jnp.concatenate([qk[:, :HEADS * LANES] * LOG2E, qk[:, HEADS * LANES:]], axis=-1)
    qk = (qk + jnp.dot(c_pieces, e_ref[...], preferred_element_type=jnp.float32)
          + qkb_ref[...]).astype(jnp.bfloat16)
    for hd in range(HEADS):
        qp_ref[0, hd] = qk[:, hd * LANES:(hd + 1) * LANES]
        kp_ref[0, hd] = qk[:, (HEADS + hd) * LANES:(HEADS + hd + 1) * LANES]

    v = jnp.dot(hn, w_ref[:, C_V:C_REST], preferred_element_type=jnp.float32)
    vt = v.T.astype(jnp.bfloat16)
    for hd in range(HEADS):
        vt_ref[0, hd] = vt[hd * HEAD_DIM:(hd + 1) * HEAD_DIM, :]

    r = jnp.dot(hn, w_ref[:, C_REST:N_PACK], preferred_element_type=jnp.float32)
    conf_a = r[:, 0:D_CONF]
    conf_g = r[:, D_CONF:2 * D_CONF]
    sc_b = r[:, 2 * D_CONF:2 * D_CONF + D_SC]
    sc_c = r[:, 2 * D_CONF + D_SC:2 * D_CONF + 2 * D_SC]
    sc_u = r[:, 2 * D_CONF + 2 * D_SC:]
    mix_ref[0, :, 0:D_CONF] = conf_a * jax.nn.sigmoid(conf_g)
    mix_ref[0, :, D_CONF:D_CONF + D_SC] = sc_b
    mix_ref[0, :, D_CONF + D_SC:] = sc_c * sc_u


def _proj_call(h, g, w_pack, bf_row, tri, e_mat, qk_bias):
    bsz, l_pad, _ = h.shape
    nt = l_pad // ROW_TILE
    const = lambda b, i: (0, 0)
    return pl.pallas_call(
        _proj_kernel,
        grid=(bsz, nt),
        in_specs=[
            pl.BlockSpec((1, ROW_TILE, D_MODEL), lambda b, i: (b, i, 0)),
            pl.BlockSpec((1, D_MODEL), const),
            pl.BlockSpec((D_MODEL, N_PACK), const),
            pl.BlockSpec((1, LANES), const),
            pl.BlockSpec((ROW_TILE, ROW_TILE), const),
            pl.BlockSpec((N_SPLIT * LANES, 2 * HEADS * LANES), const),
            pl.BlockSpec((1, 2 * HEADS * LANES), const),
        ],
        out_specs=[
            pl.BlockSpec((1, HEADS, ROW_TILE, LANES), lambda b, i: (b, 0, i, 0)),
            pl.BlockSpec((1, HEADS, ROW_TILE, LANES), lambda b, i: (b, 0, i, 0)),
            pl.BlockSpec((1, HEADS, HEAD_DIM, ROW_TILE), lambda b, i: (b, 0, 0, i)),
            pl.BlockSpec((1, ROW_TILE, D_CONF + 2 * D_SC), lambda b, i: (b, i, 0)),
        ],
        out_shape=[
            jax.ShapeDtypeStruct((bsz, HEADS, l_pad, LANES), jnp.bfloat16),
            jax.ShapeDtypeStruct((bsz, HEADS, l_pad, LANES), jnp.bfloat16),
            jax.ShapeDtypeStruct((bsz, HEADS, HEAD_DIM, l_pad), jnp.bfloat16),
            jax.ShapeDtypeStruct((bsz, l_pad, D_CONF + 2 * D_SC), jnp.float32),
        ],
        scratch_shapes=[pltpu.VMEM((8, LANES), jnp.float32)],
        compiler_params=pltpu.CompilerParams(
            dimension_semantics=("arbitrary", "arbitrary"), vmem_limit_bytes=VMEM_LIMIT),
        name="proj",
    )(h, g, w_pack, bf_row, tri, e_mat, qk_bias)


def _attn_kernel(qp_ref, kp_ref, vt_ref, o_ref, sa_ref, sb_ref):
    qi = pl.program_id(2)
    heads = range(HEADS_PER_STEP)
    qs = [qp_ref[0, hd] for hd in heads]

    def scores(j, s_ref):
        k0 = pl.multiple_of(j * BK, BK)
        for hd in heads:
            s_ref[hd] = lax.dot_general(kp_ref[0, hd, pl.ds(k0, BK), :], qs[hd], (((1,), (1,)), ((), ())),
                                        preferred_element_type=jnp.float32)

    def consume(j, s_ref, carries, diag):
        k0 = pl.multiple_of(j * BK, BK)
        out = []
        for hd in heads:
            m, l, acc = carries[hd]
            s = s_ref[hd]
            if diag is not None:
                kpos = lax.broadcasted_iota(jnp.int32, (BK, BQ), 0) + diag * BK
                qpos = lax.broadcasted_iota(jnp.int32, (BK, BQ), 1)
                s = jnp.where(kpos <= qpos, s, -jnp.inf)
            m_new = jnp.maximum(m, jnp.max(s, axis=0, keepdims=True))
            alpha = jnp.exp2(m - m_new)
            p = jnp.exp2(s - m_new)
            l = alpha * l + jnp.sum(p, axis=0, keepdims=True)
            vblk = vt_ref[0, hd, :, pl.ds(k0, BK)]
            acc = alpha * acc + jnp.dot(vblk, p.astype(jnp.bfloat16),
                                        preferred_element_type=jnp.float32)
            out.append((m_new, l, acc))
        return tuple(out)

    def pair(t, carries):
        j = 2 * t
        scores(j + 1, sb_ref)
        carries = consume(j, sa_ref, carries, None)
        scores(j + 2, sa_ref)
        return consume(j + 1, sb_ref, carries, None)

    init = (jnp.full((1, BQ), -jnp.inf, jnp.float32),
            jnp.zeros((1, BQ), jnp.float32),
            jnp.zeros((HEAD_DIM, BQ), jnp.float32))
    n_full = qi * DIAG_BLOCKS
    scores(0, sa_ref)
    carries = lax.fori_loop(0, n_full // 2, pair, (init,) * HEADS_PER_STEP)
    bufs = (sa_ref, sb_ref)
    for d in range(DIAG_BLOCKS):
        if d + 1 < DIAG_BLOCKS:
            scores(n_full + d + 1, bufs[(d + 1) % 2])
        carries = consume(n_full + d, bufs[d % 2], carries, d)
    for hd, (m, l, acc) in enumerate(carries):
        o_ref[0, hd] = (acc / l).astype(o_ref.dtype)


def _attn_call(qp, kp, vt):
    bsz, heads, l_pad, _ = qp.shape
    nq = l_pad // BQ
    hps = HEADS_PER_STEP
    return pl.pallas_call(
        _attn_kernel,
        grid=(bsz, heads // hps, nq),
        in_specs=[
            pl.BlockSpec((1, hps, BQ, LANES), lambda b, h, i: (b, h, i, 0)),
            pl.BlockSpec((1, hps, l_pad, LANES), lambda b, h, i: (b, h, 0, 0)),
            pl.BlockSpec((1, hps, HEAD_DIM, l_pad), lambda b, h, i: (b, h, 0, 0)),
        ],
        out_specs=pl.BlockSpec((1, hps, HEAD_DIM, BQ), lambda b, h, i: (b, h, 0, i)),
        out_shape=jax.ShapeDtypeStruct((bsz, heads, HEAD_DIM, l_pad), jnp.bfloat16),
        scratch_shapes=[pltpu.VMEM((hps, BK, BQ), jnp.float32), pltpu.VMEM((hps, BK, BQ), jnp.float32)],
        compiler_params=pltpu.CompilerParams(
            dimension_semantics=("arbitrary", "arbitrary", "arbitrary"), vmem_limit_bytes=VMEM_LIMIT),
        name="attn",
    )(qp, kp, vt)


def _causal_taps(ext_ref, w_ref, n_taps):
    out = None
    for k in range(n_taps):
        term = ext_ref[pl.ds(HALO - (n_taps - 1) + k, ROW_TILE), :] * w_ref[k:k + 1, :]
        out = term if out is None else out + term
    return out


def _mix_kernel(h_ref, ot_ref, mix_ref, halo_ref, wdw_ref, bdw_ref, lng_ref, lnb_ref, wpw_ref, bpw_ref,
                wsc_ref, wout_ref, ng_ref, w1_ref, w2_ref, fg_ref, out_ref, ext_a, ext_c, *, final):
    first = pl.program_id(1) == 0
    halo = jnp.where(first, 0.0, halo_ref[0])

    ext_a[0:HALO, :] = halo[:, 0:D_CONF]
    ext_a[HALO:, :] = mix_ref[0, :, 0:D_CONF]
    dw = _causal_taps(ext_a, wdw_ref, CONF_K) + bdw_ref[...]
    mu = jnp.mean(dw, axis=-1, keepdims=True)
    xc = dw - mu
    y = xc * lax.rsqrt(jnp.mean(xc * xc, axis=-1, keepdims=True) + EPS) * lng_ref[...] + lnb_ref[...]
    y = y * jax.nn.sigmoid(y)
    conf = jnp.dot(y.astype(jnp.bfloat16), wpw_ref[...], preferred_element_type=jnp.float32) + bpw_ref[...]

    ext_c[0:HALO, :] = halo[:, D_CONF + D_SC:]
    ext_c[HALO:, :] = mix_ref[0, :, D_CONF + D_SC:]
    sc = mix_ref[0, :, D_CONF:D_CONF + D_SC] * _causal_taps(ext_c, wsc_ref, SC_K)

    attn = ot_ref[0].T
    mixed = (jnp.dot(attn, wout_ref[0:D_ATTN, :], preferred_element_type=jnp.float32)
             + jnp.dot(conf.astype(jnp.bfloat16), wout_ref[D_ATTN:D_ATTN + D_CONF, :],
                       preferred_element_type=jnp.float32)
             + jnp.dot(sc.astype(jnp.bfloat16), wout_ref[D_ATTN + D_CONF:, :],
                       preferred_element_type=jnp.float32))
    h1 = h_ref[0] + mixed

    hn = _rms(h1, ng_ref[...]).astype(jnp.bfloat16)
    acc = h1
    for c0 in range(0, D_FF, FF_CHUNK):
        a = jnp.dot(hn, w1_ref[:, c0:c0 + FF_CHUNK], preferred_element_type=jnp.float32)
        a = jnp.maximum(a, 0.0)
        acc = acc + jnp.dot((a * a).astype(jnp.bfloat16), w2_ref[c0:c0 + FF_CHUNK, :],
                            preferred_element_type=jnp.float32)
    if final:
        acc = _rms(acc, fg_ref[...])
    out_ref[0] = acc


def _mix_call(h, ot, mixin, wdw, bdw, lng, lnb, wpw, bpw, wsc, wout, ng, w1, w2, fg, *, final):
    bsz, l_pad, _ = h.shape
    nt = l_pad // ROW_TILE
    const = lambda b, i: (0, 0)
    once = pl.Buffered(1)
    wide = D_CONF + 2 * D_SC
    return pl.pallas_call(
        functools.partial(_mix_kernel, final=final),
        grid=(bsz, nt),
        in_specs=[
            pl.BlockSpec((1, ROW_TILE, D_MODEL), lambda b, i: (b, i, 0)),
            pl.BlockSpec((1, D_ATTN, ROW_TILE), lambda b, i: (b, 0, i)),
            pl.BlockSpec((1, ROW_TILE, wide), lambda b, i: (b, i, 0)),
            pl.BlockSpec((1, HALO, wide),
                         lambda b, i: (b, jnp.maximum(i * (ROW_TILE // HALO) - 1, 0), 0)),
            pl.BlockSpec((HALO, D_CONF), const),
            pl.BlockSpec((1, D_CONF), const),
            pl.BlockSpec((1, D_CONF), const),
            pl.BlockSpec((1, D_CONF), const),
            pl.BlockSpec((D_CONF, D_CONF), const),
            pl.BlockSpec((1, D_CONF), const),
            pl.BlockSpec((8, D_SC), const),
            pl.BlockSpec((D_MODEL, D_MODEL), const, pipeline_mode=once),
            pl.BlockSpec((1, D_MODEL), const),
            pl.BlockSpec((D_MODEL, D_FF), const, pipeline_mode=once),
            pl.BlockSpec((D_FF, D_MODEL), const, pipeline_mode=once),
            pl.BlockSpec((1, D_MODEL), const),
        ],
        out_specs=pl.BlockSpec((1, ROW_TILE, D_MODEL), lambda b, i: (b, i, 0)),
        out_shape=jax.ShapeDtypeStruct((bsz, l_pad, D_MODEL), jnp.float32),
        scratch_shapes=[pltpu.VMEM((HALO + ROW_TILE, D_CONF), jnp.float32),
                        pltpu.VMEM((HALO + ROW_TILE, D_SC), jnp.float32)],
        compiler_params=pltpu.CompilerParams(
            dimension_semantics=("arbitrary", "arbitrary"), vmem_limit_bytes=VMEM_LIMIT),
        name="mix_final" if final else "mix",
    )(h, ot, mixin, mixin, wdw, bdw, lng, lnb, wpw, bpw, wsc, wout, ng, w1, w2, fg)


def _decay_selectors():
    e = np.zeros((N_SPLIT * LANES, 2 * HEADS * LANES), np.float32)
    bias = np.zeros((1, 2 * HEADS * LANES), np.float32)
    for hd in range(HEADS):
        for j in range(N_SPLIT):
            e[j * LANES + hd, hd * LANES + HEAD_DIM + j] = 1.0
            bias[0, hd * LANES + HEAD_DIM + N_SPLIT + j] = 1.0
            bias[0, (HEADS + hd) * LANES + HEAD_DIM + j] = 1.0
            e[j * LANES + hd, (HEADS + hd) * LANES + HEAD_DIM + N_SPLIT + j] = -1.0
    return jnp.asarray(e, jnp.bfloat16), jnp.asarray(bias, jnp.float32)


def _pack_w_in(w):
    d = w.shape[0]
    q = (w[:, 0:D_ATTN] * (HEAD_DIM ** -0.5)).reshape(d, HEADS, HEAD_DIM)
    k = w[:, D_ATTN:2 * D_ATTN].reshape(d, HEADS, HEAD_DIM)
    pad = jnp.zeros((d, HEADS, LANES - HEAD_DIM), w.dtype)
    qk = jnp.concatenate([jnp.concatenate([q, pad], -1).reshape(d, HEADS * LANES),
                          jnp.concatenate([k, pad], -1).reshape(d, HEADS * LANES)], -1)
    v = w[:, 2 * D_ATTN:3 * D_ATTN]
    f = jnp.pad(w[:, 3 * D_ATTN:3 * D_ATTN + HEADS], ((0, 0), (0, LANES - HEADS)))
    rest = w[:, 3 * D_ATTN + HEADS:]
    return jnp.concatenate([f, qk, v, rest], -1).astype(jnp.bfloat16)


def kernel(x, meta_tokens, mix_norm_g, w_in, b_forget, w_conf_dw, b_conf_dw, conf_ln_g, conf_ln_b,
           w_conf_pw, b_conf_pw, w_sc_conv, w_out, mlp_norm_g, w_mlp1, w_mlp2, final_norm_g):
    bsz, seq, d = x.shape
    depth = w_in.shape[0]
    l_real = N_META + seq
    l_pad = -(-l_real // ROW_TILE) * ROW_TILE
    assert l_pad % BQ == 0 and DIAG_BLOCKS % 2 == 0

    meta = jnp.broadcast_to(meta_tokens[None].astype(x.dtype), (bsz, N_META, d))
    h = jnp.concatenate([meta, x, jnp.zeros((bsz, l_pad - l_real, d), x.dtype)], axis=1)

    tri = jnp.asarray(np.tril(np.ones((ROW_TILE, ROW_TILE), np.float32)), jnp.bfloat16)
    e_mat, qk_bias = _decay_selectors()
    row = lambda a: a.reshape(1, -1).astype(jnp.float32)
    fg = row(final_norm_g)

    for l in range(depth):
        bf_row = jnp.pad(row(b_forget[l]), ((0, 0), (0, LANES - HEADS)))
        qp, kp, vt, mixin = _proj_call(h, row(mix_norm_g[l]), _pack_w_in(w_in[l]), bf_row, tri, e_mat, qk_bias)
        ot = _attn_call(qp, kp, vt).reshape(bsz, D_ATTN, l_pad)
        wdw = jnp.pad(w_conf_dw[l], ((0, HALO - CONF_K), (0, 0)))
        wsc = jnp.pad(w_sc_conv[l], ((0, 8 - SC_K), (0, 0)))
        h = _mix_call(h, ot, mixin, wdw, row(b_conf_dw[l]), row(conf_ln_g[l]), row(conf_ln_b[l]),
                      w_conf_pw[l].astype(jnp.bfloat16), row(b_conf_pw[l]), wsc,
                      w_out[l].astype(jnp.bfloat16), row(mlp_norm_g[l]),
                      w_mlp1[l].astype(jnp.bfloat16), w_mlp2[l].astype(jnp.bfloat16), fg,
                      final=(l == depth - 1))
    return h[:, N_META:l_real]
```

```python
import functools

import jax
import jax.numpy as jnp
import numpy as np
from jax import lax
from jax.experimental import pallas as pl
from jax.experimental.pallas import tpu as pltpu

D_MODEL = 1024
N_META = 16
HEADS = 8
HEAD_DIM = 64
D_ATTN = HEADS * HEAD_DIM
D_CONF = 256
D_SC = 256
CONF_K = 31
SC_K = 3
D_FF = 4 * D_MODEL
EPS = 1e-6

LANES = 128
ROW_TILE = 512
BQ = 512
BK = 256
DIAG_BLOCKS = BQ // BK
HEADS_PER_STEP = 2
LOG2E = 1.4426950408889634
VT_ROWS = HEAD_DIM + 16
HALO = 32
SUBLANES = 8
SHIFT_ROWS = ROW_TILE + HALO - SUBLANES
FF_CHUNK = 1024
N_SPLIT = 3
VMEM_LIMIT = 56 * 1024 * 1024

C_F = 0
C_Q = C_F + LANES
C_K = C_Q + D_ATTN
C_V = C_K + D_ATTN
C_REST = C_V + D_ATTN
N_PACK = C_REST + 2 * D_CONF + 3 * D_SC
DECAY_LANES = 2 * N_SPLIT


def _split_bf16(x):
    pieces = []
    r = x
    for _ in range(N_SPLIT):
        p = r.astype(jnp.bfloat16)
        pieces.append(p)
        r = r - p.astype(jnp.float32)
    return pieces


def _rms(x, g):
    return x * lax.rsqrt(jnp.mean(x * x, axis=-1, keepdims=True) + EPS) * g


def _proj_kernel(h_ref, g_ref, w_ref, bf_ref, tri_ref, e_ref, qkb_ref, hmask_ref,
                 qp_ref, kp_ref, vt_ref, mix_ref, carry_ref):
    @pl.when(pl.program_id(1) == 0)
    def _():
        carry_ref[...] = jnp.zeros_like(carry_ref)

    hn = _rms(h_ref[0], g_ref[...]).astype(jnp.bfloat16)

    z = jnp.dot(hn, w_ref[:, C_F:C_F + LANES], preferred_element_type=jnp.float32) + bf_ref[...]
    log_f = jnp.minimum(z, 0.0) - jnp.log1p(jnp.exp(-jnp.abs(z)))
    tri = tri_ref[...]
    c = carry_ref[0:1, :]
    for piece in _split_bf16(log_f):
        c = c + jnp.dot(tri, piece, preferred_element_type=jnp.float32)
    carry_ref[0:1, :] = c[ROW_TILE - 1:ROW_TILE, :]

    c_pieces = jnp.concatenate(_split_bf16(c * LOG2E), axis=-1)
    decay = jnp.dot(c_pieces, e_ref[...], preferred_element_type=jnp.float32) + qkb_ref[...]
    decay_q, decay_k = decay[:, :LANES], decay[:, LANES:]
    q = jnp.dot(hn, w_ref[:, C_Q:C_K], preferred_element_type=jnp.float32) * LOG2E
    k = jnp.dot(hn, w_ref[:, C_K:C_V], preferred_element_type=jnp.float32)
    low_half = lax.broadcasted_iota(jnp.int32, (ROW_TILE, LANES), 1) < HEAD_DIM
    for hd in range(HEADS):
        pair_cols = slice((hd // 2) * LANES, (hd // 2 + 1) * LANES)
        q_h, k_h = q[:, pair_cols], k[:, pair_cols]
        if hd % 2:
            q_h, k_h = pltpu.roll(q_h, HEAD_DIM, axis=1), pltpu.roll(k_h, HEAD_DIM, axis=1)
        qp_ref[0, hd] = jnp.where(low_half, q_h, decay_q).astype(jnp.bfloat16)
        kp_ref[0, hd] = jnp.where(low_half, k_h, decay_k * hmask_ref[hd:hd + 1, :]).astype(jnp.bfloat16)

    v = jnp.dot(hn, w_ref[:, C_V:C_REST], preferred_element_type=jnp.float32)
    vt = v.T.astype(jnp.bfloat16)
    ones_row = (lax.broadcasted_iota(jnp.int32, (VT_ROWS - HEAD_DIM, ROW_TILE), 0) == 0).astype(jnp.bfloat16)
    for hd in range(HEADS):
        vt_ref[0, hd, 0:HEAD_DIM, :] = vt[hd * HEAD_DIM:(hd + 1) * HEAD_DIM, :]
        vt_ref[0, hd, HEAD_DIM:, :] = ones_row

    r = jnp.dot(hn, w_ref[:, C_REST:N_PACK], preferred_element_type=jnp.float32)
    conf_a = r[:, 0:D_CONF]
    conf_g = r[:, D_CONF:2 * D_CONF]
    sc_b = r[:, 2 * D_CONF:2 * D_CONF + D_SC]
    sc_c = r[:, 2 * D_CONF + D_SC:2 * D_CONF + 2 * D_SC]
    sc_u = r[:, 2 * D_CONF + 2 * D_SC:]
    mix_ref[0, :, 0:D_CONF] = conf_a * jax.nn.sigmoid(conf_g)
    mix_ref[0, :, D_CONF:D_CONF + D_SC] = sc_b
    mix_ref[0, :, D_CONF + D_SC:] = sc_c * sc_u


def _proj_call(h, g, w_pack, bf_row, tri, e_mat, qk_bias, head_mask):
    bsz, l_pad, _ = h.shape
    nt = l_pad // ROW_TILE
    const = lambda b, i: (0, 0)
    return pl.pallas_call(
        _proj_kernel,
        grid=(bsz, nt),
        in_specs=[
            pl.BlockSpec((1, ROW_TILE, D_MODEL), lambda b, i: (b, i, 0)),
            pl.BlockSpec((1, D_MODEL), const),
            pl.BlockSpec((D_MODEL, N_PACK), const),
            pl.BlockSpec((1, LANES), const),
            pl.BlockSpec((ROW_TILE, ROW_TILE), const),
            pl.BlockSpec((N_SPLIT * LANES, 2 * LANES), const),
            pl.BlockSpec((1, 2 * LANES), const),
            pl.BlockSpec((HEADS, LANES), const),
        ],
        out_specs=[
            pl.BlockSpec((1, HEADS, ROW_TILE, LANES), lambda b, i: (b, 0, i, 0)),
            pl.BlockSpec((1, HEADS, ROW_TILE, LANES), lambda b, i: (b, 0, i, 0)),
            pl.BlockSpec((1, HEADS, VT_ROWS, ROW_TILE), lambda b, i: (b, 0, 0, i)),
            pl.BlockSpec((1, ROW_TILE, D_CONF + 2 * D_SC), lambda b, i: (b, i, 0)),
        ],
        out_shape=[
            jax.ShapeDtypeStruct((bsz, HEADS, l_pad, LANES), jnp.bfloat16),
            jax.ShapeDtypeStruct((bsz, HEADS, l_pad, LANES), jnp.bfloat16),
            jax.ShapeDtypeStruct((bsz, HEADS, VT_ROWS, l_pad), jnp.bfloat16),
            jax.ShapeDtypeStruct((bsz, l_pad, D_CONF + 2 * D_SC), jnp.float32),
        ],
        scratch_shapes=[pltpu.VMEM((8, LANES), jnp.float32)],
        compiler_params=pltpu.CompilerParams(
            dimension_semantics=("arbitrary", "arbitrary"), vmem_limit_bytes=VMEM_LIMIT),
        name="proj",
    )(h, g, w_pack, bf_row, tri, e_mat, qk_bias, head_mask)


def _attn_kernel(qp_ref, kp_ref, vt_ref, o_ref, sa_ref, sb_ref):
    qi = pl.program_id(2)
    heads = range(HEADS_PER_STEP)
    qs = [qp_ref[0, hd] for hd in heads]

    def scores(j, s_ref):
        k0 = pl.multiple_of(j * BK, BK)
        for hd in heads:
            s_ref[hd] = lax.dot_general(kp_ref[0, hd, pl.ds(k0, BK), :], qs[hd], (((1,), (1,)), ((), ())),
                                        preferred_element_type=jnp.float32)

    def consume(j, s_ref, carries, diag):
        k0 = pl.multiple_of(j * BK, BK)
        out = []
        for hd in heads:
            m, acc = carries[hd]
            s = s_ref[hd]
            if diag is not None:
                kpos = lax.broadcasted_iota(jnp.int32, (BK, BQ), 0) + diag * BK
                qpos = lax.broadcasted_iota(jnp.int32, (BK, BQ), 1)
                s = jnp.where(kpos <= qpos, s, -jnp.inf)
            m_new = jnp.maximum(m, jnp.max(s, axis=0, keepdims=True))
            p = jnp.exp2((s - m_new).astype(jnp.bfloat16))
            vblk = vt_ref[0, hd, :, pl.ds(k0, BK)]
            acc = jnp.exp2(m - m_new) * acc + jnp.dot(vblk, p, preferred_element_type=jnp.float32)
            out.append((m_new, acc))
        return tuple(out)

    def pair(t, carries):
        j = 2 * t
        scores(j + 1, sb_ref)
        carries = consume(j, sa_ref, carries, None)
        scores(j + 2, sa_ref)
        return consume(j + 1, sb_ref, carries, None)

    init = (jnp.full((1, BQ), -jnp.inf, jnp.float32), jnp.zeros((VT_ROWS, BQ), jnp.float32))
    n_full = qi * DIAG_BLOCKS
    scores(0, sa_ref)
    carries = lax.fori_loop(0, n_full // 2, pair, (init,) * HEADS_PER_STEP)
    scores(n_full + 1, sb_ref)
    carries = consume(n_full, sa_ref, carries, 0)
    carries = consume(n_full + 1, sb_ref, carries, 1)
    for hd, (m, acc) in enumerate(carries):
        o_ref[0, hd] = (acc[0:HEAD_DIM] / acc[HEAD_DIM:HEAD_DIM + 1]).astype(o_ref.dtype)


def _attn_call(qp, kp, vt):
    bsz, heads, l_pad, _ = qp.shape
    nq = l_pad // BQ
    hps = HEADS_PER_STEP
    return pl.pallas_call(
        _attn_kernel,
        grid=(bsz, heads // hps, nq),
        in_specs=[
            pl.BlockSpec((1, hps, BQ, LANES), lambda b, h, i: (b, h, i, 0)),
            pl.BlockSpec((1, hps, l_pad, LANES), lambda b, h, i: (b, h, 0, 0)),
            pl.BlockSpec((1, hps, VT_ROWS, l_pad), lambda b, h, i: (b, h, 0, 0)),
        ],
        out_specs=pl.BlockSpec((1, hps, HEAD_DIM, BQ), lambda b, h, i: (b, h, 0, i)),
        out_shape=jax.ShapeDtypeStruct((bsz, heads, HEAD_DIM, l_pad), jnp.bfloat16),
        scratch_shapes=[pltpu.VMEM((hps, BK, BQ), jnp.float32), pltpu.VMEM((hps, BK, BQ), jnp.float32)],
        compiler_params=pltpu.CompilerParams(
            dimension_semantics=("arbitrary", "arbitrary", "arbitrary"), vmem_limit_bytes=VMEM_LIMIT),
        name="attn",
    )(qp, kp, vt)


def _causal_taps(ext_ref, w_ref, n_taps, shift_ref=None):
    if shift_ref is not None:
        for r in range(1, SUBLANES):
            shift_ref[r - 1] = ext_ref[pl.ds(r, SHIFT_ROWS), :]
    out = None
    for k in range(n_taps):
        start = HALO - (n_taps - 1) + k
        if shift_ref is None or start % SUBLANES == 0:
            window = ext_ref[pl.ds(start, ROW_TILE), :]
        else:
            window = shift_ref[start % SUBLANES - 1, pl.ds(start - start % SUBLANES, ROW_TILE), :]
        term = window * w_ref[k:k + 1, :]
        out = term if out is None else out + term
    return out


def _mix_kernel(h_ref, ot_ref, mix_ref, halo_ref, wdw_ref, bdw_ref, lng_ref, lnb_ref, wpw_ref, bpw_ref,
                wsc_ref, wout_ref, ng_ref, w1_ref, w2_ref, fg_ref, out_ref, ext_a, ext_c, shift_a, *, final):
    first = pl.program_id(1) == 0
    halo = jnp.where(first, 0.0, halo_ref[0])

    ext_a[0:HALO, :] = halo[:, 0:D_CONF]
    ext_a[HALO:, :] = mix_ref[0, :, 0:D_CONF]
    dw = _causal_taps(ext_a, wdw_ref, CONF_K, shift_a) + bdw_ref[...]
    mu = jnp.mean(dw, axis=-1, keepdims=True)
    xc = dw - mu
    y = xc * lax.rsqrt(jnp.mean(xc * xc, axis=-1, keepdims=True) + EPS) * lng_ref[...] + lnb_ref[...]
    y = y * jax.nn.sigmoid(y)
    conf = jnp.dot(y.astype(jnp.bfloat16), wpw_ref[...], preferred_element_type=jnp.float32) + bpw_ref[...]

    ext_c[0:HALO, :] = halo[:, D_CONF + D_SC:]
    ext_c[HALO:, :] = mix_ref[0, :, D_CONF + D_SC:]
    sc = mix_ref[0, :, D_CONF:D_CONF + D_SC] * _causal_taps(ext_c, wsc_ref, SC_K)

    attn = ot_ref[0].T
    mixed = (jnp.dot(attn, wout_ref[0:D_ATTN, :], preferred_element_type=jnp.float32)
             + jnp.dot(conf.astype(jnp.bfloat16), wout_ref[D_ATTN:D_ATTN + D_CONF, :],
                       preferred_element_type=jnp.float32)
             + jnp.dot(sc.astype(jnp.bfloat16), wout_ref[D_ATTN + D_CONF:, :],
                       preferred_element_type=jnp.float32))
    h1 = h_ref[0] + mixed

    hn = _rms(h1, ng_ref[...]).astype(jnp.bfloat16)
    acc = h1
    for c0 in range(0, D_FF, FF_CHUNK):
        a = jnp.dot(hn, w1_ref[:, c0:c0 + FF_CHUNK], preferred_element_type=jnp.float32)
        a = jnp.maximum(a, 0.0)
        acc = acc + jnp.dot((a * a).astype(jnp.bfloat16), w2_ref[c0:c0 + FF_CHUNK, :],
                            preferred_element_type=jnp.float32)
    if final:
        acc = _rms(acc, fg_ref[...])
    out_ref[0] = acc


def _mix_call(h, ot, mixin, wdw, bdw, lng, lnb, wpw, bpw, wsc, wout, ng, w1, w2, fg, *, final):
    bsz, l_pad, _ = h.shape
    nt = l_pad // ROW_TILE
    const = lambda b, i: (0, 0)
    once = pl.Buffered(1)
    wide = D_CONF + 2 * D_SC
    return pl.pallas_call(
        functools.partial(_mix_kernel, final=final),
        grid=(bsz, nt),
        in_specs=[
            pl.BlockSpec((1, ROW_TILE, D_MODEL), lambda b, i: (b, i, 0)),
            pl.BlockSpec((1, D_ATTN, ROW_TILE), lambda b, i: (b, 0, i)),
            pl.BlockSpec((1, ROW_TILE, wide), lambda b, i: (b, i, 0)),
            pl.BlockSpec((1, HALO, wide),
                         lambda b, i: (b, jnp.maximum(i * (ROW_TILE // HALO) - 1, 0), 0)),
            pl.BlockSpec((HALO, D_CONF), const),
            pl.BlockSpec((1, D_CONF), const),
            pl.BlockSpec((1, D_CONF), const),
            pl.BlockSpec((1, D_CONF), const),
            pl.BlockSpec((D_CONF, D_CONF), const),
            pl.BlockSpec((1, D_CONF), const),
            pl.BlockSpec((8, D_SC), const),
            pl.BlockSpec((D_MODEL, D_MODEL), const, pipeline_mode=once),
            pl.BlockSpec((1, D_MODEL), const),
            pl.BlockSpec((D_MODEL, D_FF), const, pipeline_mode=once),
            pl.BlockSpec((D_FF, D_MODEL), const, pipeline_mode=once),
            pl.BlockSpec((1, D_MODEL), const),
        ],
        out_specs=pl.BlockSpec((1, ROW_TILE, D_MODEL), lambda b, i: (b, i, 0)),
        out_shape=jax.ShapeDtypeStruct((bsz, l_pad, D_MODEL), jnp.float32),
        scratch_shapes=[pltpu.VMEM((HALO + ROW_TILE, D_CONF), jnp.float32),
                        pltpu.VMEM((HALO + ROW_TILE, D_SC), jnp.float32),
                        pltpu.VMEM((SUBLANES - 1, SHIFT_ROWS, D_CONF), jnp.float32)],
        compiler_params=pltpu.CompilerParams(
            dimension_semantics=("arbitrary", "arbitrary"), vmem_limit_bytes=VMEM_LIMIT),
        name="mix_final" if final else "mix",
    )(h, ot, mixin, mixin, wdw, bdw, lng, lnb, wpw, bpw, wsc, wout, ng, w1, w2, fg)


def _decay_selectors():
    assert HEAD_DIM + HEADS * DECAY_LANES <= LANES
    e = np.zeros((N_SPLIT * LANES, 2 * LANES), np.float32)
    bias = np.zeros((1, 2 * LANES), np.float32)
    mask = np.zeros((HEADS, LANES), np.float32)
    for hd in range(HEADS):
        lane0 = HEAD_DIM + hd * DECAY_LANES
        mask[hd, lane0:lane0 + DECAY_LANES] = 1.0
        for j in range(N_SPLIT):
            e[j * LANES + hd, lane0 + j] = 1.0
            bias[0, lane0 + N_SPLIT + j] = 1.0
            bias[0, LANES + lane0 + j] = 1.0
            e[j * LANES + hd, LANES + lane0 + N_SPLIT + j] = -1.0
    return jnp.asarray(e, jnp.bfloat16), jnp.asarray(bias, jnp.float32), jnp.asarray(mask, jnp.float32)


def _pack_w_in(w):
    q = w[:, 0:D_ATTN] * (HEAD_DIM ** -0.5)
    k = w[:, D_ATTN:2 * D_ATTN]
    v = w[:, 2 * D_ATTN:3 * D_ATTN]
    f = jnp.pad(w[:, 3 * D_ATTN:3 * D_ATTN + HEADS], ((0, 0), (0, LANES - HEADS)))
    rest = w[:, 3 * D_ATTN + HEADS:]
    return jnp.concatenate([f, q, k, v, rest], -1).astype(jnp.bfloat16)


def kernel(x, meta_tokens, mix_norm_g, w_in, b_forget, w_conf_dw, b_conf_dw, conf_ln_g, conf_ln_b,
           w_conf_pw, b_conf_pw, w_sc_conv, w_out, mlp_norm_g, w_mlp1, w_mlp2, final_norm_g):
    bsz, seq, d = x.shape
    depth = w_in.shape[0]
    l_real = N_META + seq
    l_pad = -(-l_real // ROW_TILE) * ROW_TILE
    assert l_pad % BQ == 0 and DIAG_BLOCKS == 2

    meta = jnp.broadcast_to(meta_tokens[None].astype(x.dtype), (bsz, N_META, d))
    h = jnp.concatenate([meta, x, jnp.zeros((bsz, l_pad - l_real, d), x.dtype)], axis=1)

    tri = jnp.asarray(np.tril(np.ones((ROW_TILE, ROW_TILE), np.float32)), jnp.bfloat16)
    e_mat, qk_bias, head_mask = _decay_selectors()
    row = lambda a: a.reshape(1, -1).astype(jnp.float32)
    fg = row(final_norm_g)

    for l in range(depth):
        bf_row = jnp.pad(row(b_forget[l]), ((0, 0), (0, LANES - HEADS)))
        qp, kp, vt, mixin = _proj_call(h, row(mix_norm_g[l]), _pack_w_in(w_in[l]), bf_row, tri, e_mat, qk_bias,
                                       head_mask)
        ot = _attn_call(qp, kp, vt).reshape(bsz, D_ATTN, l_pad)
        wdw = jnp.pad(w_conf_dw[l], ((0, HALO - CONF_K), (0, 0)))
        wsc = jnp.pad(w_sc_conv[l], ((0, 8 - SC_K), (0, 0)))
        h = _mix_call(h, ot, mixin, wdw, row(b_conf_dw[l]), row(conf_ln_g[l]), row(conf_ln_b[l]),
                      w_conf_pw[l].astype(jnp.bfloat16), row(b_conf_pw[l]), wsc,
                      w_out[l].astype(jnp.bfloat16), row(mlp_norm_g[l]),
                      w_mlp1[l].astype(jnp.bfloat16), w_mlp2[l].astype(jnp.bfloat16), fg,
                      final=(l == depth - 1))
    return h[:, N_META:l_real]
```

```python
import functools

import jax
import jax.numpy as jnp
import numpy as np
from jax import lax
from jax.experimental import pallas as pl
from jax.experimental.pallas import tpu as pltpu

D_MODEL = 1024
N_META = 16
HEADS = 8
HEAD_DIM = 64
D_ATTN = HEADS * HEAD_DIM
D_CONF = 256
D_SC = 256
CONF_K = 31
SC_K = 3
D_FF = 4 * D_MODEL
EPS = 1e-6

LANES = 128
SUBLANES = 8
ROW_TILE = 512
PAD_FRONT = ROW_TILE - N_META
BQ = ROW_TILE
BK = 256
HEADS_PER_STEP = 2
LOG2E = 1.4426950408889634
PAD_KEY_BIAS = -2.0 ** 100
HALO = 32
SHIFT_ROWS = ROW_TILE + HALO - SUBLANES
FF_CHUNK = 1024
N_SPLIT = 3
VMEM_LIMIT = 56 * 1024 * 1024

C_F = 0
C_Q = C_F + LANES
C_K = C_Q + D_ATTN
C_V = C_K + D_ATTN
C_REST = C_V + D_ATTN
N_PACK = C_REST + 2 * D_CONF + 3 * D_SC
DECAY_LANES = 2 * N_SPLIT


def _split_bf16(x):
    pieces = []
    r = x
    for _ in range(N_SPLIT):
        p = r.astype(jnp.bfloat16)
        pieces.append(p)
        r = r - p.astype(jnp.float32)
    return pieces


def _rms(x, g):
    return x * lax.rsqrt(jnp.mean(x * x, axis=-1, keepdims=True) + EPS) * g


def _tile_rows(src_ref, meta_ref, first):
    h = src_ref[0]
    if first:
        h = jnp.where(pl.program_id(1) == 0, meta_ref[...], h)
    return h


def _pad_rows_below(shape):
    limit = jnp.where(pl.program_id(1) == 0, PAD_FRONT, 0)
    return lax.broadcasted_iota(jnp.int32, shape, 0) < limit


def _proj_kernel(src_ref, meta_ref, g_ref, w_ref, bf_ref, tri_ref, e_ref, qkb_ref, lanes_ref,
                 qp_ref, kp_ref, vt_ref, mix_ref, carry_ref, *, first):
    @pl.when(pl.program_id(1) == 0)
    def _():
        carry_ref[...] = jnp.zeros_like(carry_ref)

    hn = _rms(_tile_rows(src_ref, meta_ref, first), g_ref[...]).astype(jnp.bfloat16)

    z = jnp.dot(hn, w_ref[:, C_F:C_F + LANES], preferred_element_type=jnp.float32) + bf_ref[...]
    log_f = jnp.minimum(z, 0.0) - jnp.log1p(jnp.exp(-jnp.abs(z)))
    tri = tri_ref[...]
    c = carry_ref[0:1, :]
    for piece in _split_bf16(log_f):
        c = c + jnp.dot(tri, piece, preferred_element_type=jnp.float32)
    carry_ref[0:1, :] = c[ROW_TILE - 1:ROW_TILE, :]

    c_pieces = jnp.concatenate(_split_bf16(c * LOG2E), axis=-1)
    decay = jnp.dot(c_pieces, e_ref[...], preferred_element_type=jnp.float32) + qkb_ref[...]
    decay_q, decay_k = decay[:, :LANES], decay[:, LANES:]
    decay_k = decay_k + jnp.where(_pad_rows_below((ROW_TILE, LANES)), lanes_ref[HEADS:HEADS + 1, :], 0.0)
    q = jnp.dot(hn, w_ref[:, C_Q:C_K], preferred_element_type=jnp.float32) * LOG2E
    k = jnp.dot(hn, w_ref[:, C_K:C_V], preferred_element_type=jnp.float32)
    low_half = lax.broadcasted_iota(jnp.int32, (ROW_TILE, LANES), 1) < HEAD_DIM
    for hd in range(HEADS):
        pair_cols = slice((hd // 2) * LANES, (hd // 2 + 1) * LANES)
        q_h, k_h = q[:, pair_cols], k[:, pair_cols]
        if hd % 2:
            q_h, k_h = pltpu.roll(q_h, HEAD_DIM, axis=1), pltpu.roll(k_h, HEAD_DIM, axis=1)
        qp_ref[0, hd] = jnp.where(low_half, q_h, decay_q).astype(jnp.bfloat16)
        kp_ref[0, hd] = jnp.where(low_half, k_h, decay_k * lanes_ref[hd:hd + 1, :]).astype(jnp.bfloat16)

    v = jnp.dot(hn, w_ref[:, C_V:C_REST], preferred_element_type=jnp.float32)
    vt = v.T.astype(jnp.bfloat16)
    for hd in range(HEADS):
        vt_ref[0, hd] = vt[hd * HEAD_DIM:(hd + 1) * HEAD_DIM, :]

    r = jnp.dot(hn, w_ref[:, C_REST:N_PACK], preferred_element_type=jnp.float32)
    conf_a = r[:, 0:D_CONF]
    conf_g = r[:, D_CONF:2 * D_CONF]
    sc_b = r[:, 2 * D_CONF:2 * D_CONF + D_SC]
    sc_c = r[:, 2 * D_CONF + D_SC:2 * D_CONF + 2 * D_SC]
    sc_u = r[:, 2 * D_CONF + 2 * D_SC:]
    mix_ref[0, :, 0:D_CONF] = conf_a * jax.nn.sigmoid(conf_g)
    mix_ref[0, :, D_CONF:D_CONF + D_SC] = sc_b
    mix_ref[0, :, D_CONF + D_SC:] = sc_c * sc_u


def _src_spec(first):
    if first:
        return pl.BlockSpec((1, ROW_TILE, D_MODEL), lambda b, i: (b, jnp.maximum(i - 1, 0), 0))
    return pl.BlockSpec((1, ROW_TILE, D_MODEL), lambda b, i: (b, i, 0))


def _proj_call(src, meta_tile, g, w_pack, bf_row, tri, e_mat, qk_bias, lane_rows, *, first, l_pad):
    bsz = src.shape[0]
    nt = l_pad // ROW_TILE
    const = lambda b, i: (0, 0)
    return pl.pallas_call(
        functools.partial(_proj_kernel, first=first),
        grid=(bsz, nt),
        in_specs=[
            _src_spec(first),
            pl.BlockSpec((ROW_TILE, D_MODEL), const),
            pl.BlockSpec((1, D_MODEL), const),
            pl.BlockSpec((D_MODEL, N_PACK), const),
            pl.BlockSpec((1, LANES), const),
            pl.BlockSpec((ROW_TILE, ROW_TILE), const),
            pl.BlockSpec((N_SPLIT * LANES, 2 * LANES), const),
            pl.BlockSpec((1, 2 * LANES), const),
            pl.BlockSpec((2 * HEADS, LANES), const),
        ],
        out_specs=[
            pl.BlockSpec((1, HEADS, ROW_TILE, LANES), lambda b, i: (b, 0, i, 0)),
            pl.BlockSpec((1, HEADS, ROW_TILE, LANES), lambda b, i: (b, 0, i, 0)),
            pl.BlockSpec((1, HEADS, HEAD_DIM, ROW_TILE), lambda b, i: (b, 0, 0, i)),
            pl.BlockSpec((1, ROW_TILE, D_CONF + 2 * D_SC), lambda b, i: (b, i, 0)),
        ],
        out_shape=[
            jax.ShapeDtypeStruct((bsz, HEADS, l_pad, LANES), jnp.bfloat16),
            jax.ShapeDtypeStruct((bsz, HEADS, l_pad, LANES), jnp.bfloat16),
            jax.ShapeDtypeStruct((bsz, HEADS, HEAD_DIM, l_pad), jnp.bfloat16),
            jax.ShapeDtypeStruct((bsz, l_pad, D_CONF + 2 * D_SC), jnp.float32),
        ],
        scratch_shapes=[pltpu.VMEM((8, LANES), jnp.float32)],
        compiler_params=pltpu.CompilerParams(
            dimension_semantics=("arbitrary", "arbitrary"), vmem_limit_bytes=VMEM_LIMIT),
        name="proj_first" if first else "proj",
    )(src, meta_tile, g, w_pack, bf_row, tri, e_mat, qk_bias, lane_rows)


def _attn_kernel(qp_ref, kp_ref, vt_ref, o_ref, sa_ref, sb_ref):
    qi = pl.program_id(2)
    heads = range(HEADS_PER_STEP)
    qs = [qp_ref[0, hd] for hd in heads]

    def scores(j, s_ref):
        k0 = pl.multiple_of(j * BK, BK)
        for hd in heads:
            s_ref[hd] = lax.dot_general(kp_ref[0, hd, pl.ds(k0, BK), :], qs[hd], (((1,), (1,)), ((), ())),
                                        preferred_element_type=jnp.float32)

    def consume(j, s_ref, carries, diag):
        k0 = pl.multiple_of(j * BK, BK)
        out = []
        for hd in heads:
            m, l, acc = carries[hd]
            s = s_ref[hd]
            if diag is not None:
                kpos = lax.broadcasted_iota(jnp.int32, (BK, BQ), 0) + diag * BK
                qpos = lax.broadcasted_iota(jnp.int32, (BK, BQ), 1)
                s = jnp.where(kpos <= qpos, s, -jnp.inf)
            m_new = jnp.maximum(m, jnp.max(s, axis=0, keepdims=True))
            alpha = jnp.exp2(m - m_new)
            p = jnp.exp2(s - m_new)
            l = alpha * l + jnp.sum(p, axis=0, keepdims=True)
            vblk = vt_ref[0, hd, :, pl.ds(k0, BK)]
            acc = alpha * acc + jnp.dot(vblk, p.astype(jnp.bfloat16),
                                        preferred_element_type=jnp.float32)
            out.append((m_new, l, acc))
        return tuple(out)

    def finish(carries):
        for hd, (m, l, acc) in enumerate(carries):
            o_ref[0, hd] = (acc / l).astype(o_ref.dtype)

    init = ((jnp.full((1, BQ), -jnp.inf, jnp.float32), jnp.zeros((1, BQ), jnp.float32),
             jnp.zeros((HEAD_DIM, BQ), jnp.float32)),) * HEADS_PER_STEP

    @pl.when(qi == 0)
    def _():
        scores(0, sa_ref)
        scores(1, sb_ref)
        finish(consume(1, sb_ref, consume(0, sa_ref, init, 0), 1))

    @pl.when(qi > 0)
    def _():
        def pair(t, carries):
            j = 2 * t + 1
            scores(j + 1, sb_ref)
            carries = consume(j, sa_ref, carries, None)
            scores(j + 2, sa_ref)
            return consume(j + 1, sb_ref, carries, None)

        scores(1, sa_ref)
        carries = lax.fori_loop(0, qi - 1, pair, init)
        last = 2 * qi - 1
        scores(last + 1, sb_ref)
        carries = consume(last, sa_ref, carries, None)
        scores(last + 2, sa_ref)
        carries = consume(last + 1, sb_ref, carries, 0)
        finish(consume(last + 2, sa_ref, carries, 1))


def _attn_call(qp, kp, vt):
    bsz, heads, l_pad, _ = qp.shape
    nq = l_pad // BQ
    hps = HEADS_PER_STEP
    return pl.pallas_call(
        _attn_kernel,
        grid=(bsz, heads // hps, nq),
        in_specs=[
            pl.BlockSpec((1, hps, BQ, LANES), lambda b, h, i: (b, h, i, 0)),
            pl.BlockSpec((1, hps, l_pad, LANES), lambda b, h, i: (b, h, 0, 0)),
            pl.BlockSpec((1, hps, HEAD_DIM, l_pad), lambda b, h, i: (b, h, 0, 0)),
        ],
        out_specs=pl.BlockSpec((1, hps, HEAD_DIM, BQ), lambda b, h, i: (b, h, 0, i)),
        out_shape=jax.ShapeDtypeStruct((bsz, heads, HEAD_DIM, l_pad), jnp.bfloat16),
        scratch_shapes=[pltpu.VMEM((hps, BK, BQ), jnp.float32), pltpu.VMEM((hps, BK, BQ), jnp.float32)],
        compiler_params=pltpu.CompilerParams(
            dimension_semantics=("arbitrary", "arbitrary", "arbitrary"), vmem_limit_bytes=VMEM_LIMIT),
        name="attn",
    )(qp, kp, vt)


def _causal_taps(ext_ref, w_ref, n_taps, shift_ref=None):
    if shift_ref is not None:
        for r in range(1, SUBLANES):
            shift_ref[r - 1] = ext_ref[pl.ds(r, SHIFT_ROWS), :]
    out = None
    for k in range(n_taps):
        start = HALO - (n_taps - 1) + k
        if shift_ref is None or start % SUBLANES == 0:
            window = ext_ref[pl.ds(start, ROW_TILE), :]
        else:
            window = shift_ref[start % SUBLANES - 1, pl.ds(start - start % SUBLANES, ROW_TILE), :]
        term = window * w_ref[k:k + 1, :]
        out = term if out is None else out + term
    return out


def _mix_kernel(src_ref, meta_ref, ot_ref, mix_ref, halo_ref, wdw_ref, bdw_ref, lng_ref, lnb_ref, wpw_ref,
                bpw_ref, wsc_ref, wout_ref, ng_ref, w1_ref, w2_ref, fg_ref, out_ref, ext_a, ext_c, shift_a,
                *, first, final):
    halo = jnp.where(pl.program_id(1) == 0, 0.0, halo_ref[0])

    ext_a[0:HALO, :] = halo[:, 0:D_CONF]
    ext_a[HALO:, :] = mix_ref[0, :, 0:D_CONF]
    dw = _causal_taps(ext_a, wdw_ref, CONF_K, shift_a) + bdw_ref[...]
    mu = jnp.mean(dw, axis=-1, keepdims=True)
    xc = dw - mu
    y = xc * lax.rsqrt(jnp.mean(xc * xc, axis=-1, keepdims=True) + EPS) * lng_ref[...] + lnb_ref[...]
    y = y * jax.nn.sigmoid(y)
    conf = jnp.dot(y.astype(jnp.bfloat16), wpw_ref[...], preferred_element_type=jnp.float32) + bpw_ref[...]

    ext_c[0:HALO, :] = halo[:, D_CONF + D_SC:]
    ext_c[HALO:, :] = mix_ref[0, :, D_CONF + D_SC:]
    sc = mix_ref[0, :, D_CONF:D_CONF + D_SC] * _causal_taps(ext_c, wsc_ref, SC_K)

    attn = ot_ref[0].T
    mixed = (jnp.dot(attn, wout_ref[0:D_ATTN, :], preferred_element_type=jnp.float32)
             + jnp.dot(conf.astype(jnp.bfloat16), wout_ref[D_ATTN:D_ATTN + D_CONF, :],
                       preferred_element_type=jnp.float32)
             + jnp.dot(sc.astype(jnp.bfloat16), wout_ref[D_ATTN + D_CONF:, :],
                       preferred_element_type=jnp.float32))
    h1 = _tile_rows(src_ref, meta_ref, first) + mixed

    hn = _rms(h1, ng_ref[...]).astype(jnp.bfloat16)
    acc = h1
    for c0 in range(0, D_FF, FF_CHUNK):
        a = jnp.dot(hn, w1_ref[:, c0:c0 + FF_CHUNK], preferred_element_type=jnp.float32)
        a = jnp.maximum(a, 0.0)
        acc = acc + jnp.dot((a * a).astype(jnp.bfloat16), w2_ref[c0:c0 + FF_CHUNK, :],
                            preferred_element_type=jnp.float32)
    if final:
        out_ref[0] = _rms(acc, fg_ref[...])
    else:
        out_ref[0] = jnp.where(_pad_rows_below(acc.shape), 0.0, acc)


def _mix_call(src, meta_tile, ot, mixin, wdw, bdw, lng, lnb, wpw, bpw, wsc, wout, ng, w1, w2, fg,
              *, first, final, l_pad):
    bsz = src.shape[0]
    nt = l_pad // ROW_TILE
    const = lambda b, i: (0, 0)
    once = pl.Buffered(1)
    wide = D_CONF + 2 * D_SC
    if final:
        out_spec = pl.BlockSpec((1, ROW_TILE, D_MODEL), lambda b, i: (b, jnp.maximum(i - 1, 0), 0))
        out_shape = jax.ShapeDtypeStruct((bsz, l_pad - ROW_TILE, D_MODEL), jnp.float32)
    else:
        out_spec = pl.BlockSpec((1, ROW_TILE, D_MODEL), lambda b, i: (b, i, 0))
        out_shape = jax.ShapeDtypeStruct((bsz, l_pad, D_MODEL), jnp.float32)
    return pl.pallas_call(
        functools.partial(_mix_kernel, first=first, final=final),
        grid=(bsz, nt),
        in_specs=[
            _src_spec(first),
            pl.BlockSpec((ROW_TILE, D_MODEL), const),
            pl.BlockSpec((1, D_ATTN, ROW_TILE), lambda b, i: (b, 0, i)),
            pl.BlockSpec((1, ROW_TILE, wide), lambda b, i: (b, i, 0)),
            pl.BlockSpec((1, HALO, wide),
                         lambda b, i: (b, jnp.maximum(i * (ROW_TILE // HALO) - 1, 0), 0)),
            pl.BlockSpec((HALO, D_CONF), const),
            pl.BlockSpec((1, D_CONF), const),
            pl.BlockSpec((1, D_CONF), const),
            pl.BlockSpec((1, D_CONF), const),
            pl.BlockSpec((D_CONF, D_CONF), const),
            pl.BlockSpec((1, D_CONF), const),
            pl.BlockSpec((8, D_SC), const),
            pl.BlockSpec((D_MODEL, D_MODEL), const, pipeline_mode=once),
            pl.BlockSpec((1, D_MODEL), const),
            pl.BlockSpec((D_MODEL, D_FF), const, pipeline_mode=once),
            pl.BlockSpec((D_FF, D_MODEL), const, pipeline_mode=once),
            pl.BlockSpec((1, D_MODEL), const),
        ],
        out_specs=out_spec,
        out_shape=out_shape,
        scratch_shapes=[pltpu.VMEM((HALO + ROW_TILE, D_CONF), jnp.float32),
                        pltpu.VMEM((HALO + ROW_TILE, D_SC), jnp.float32),
                        pltpu.VMEM((SUBLANES - 1, SHIFT_ROWS, D_CONF), jnp.float32)],
        compiler_params=pltpu.CompilerParams(
            dimension_semantics=("arbitrary", "arbitrary"), vmem_limit_bytes=VMEM_LIMIT),
        name="mix_final" if final else "mix",
    )(src, meta_tile, ot, mixin, mixin, wdw, bdw, lng, lnb, wpw, bpw, wsc, wout, ng, w1, w2, fg)


def _decay_selectors():
    assert HEAD_DIM + HEADS * DECAY_LANES <= LANES
    e = np.zeros((N_SPLIT * LANES, 2 * LANES), np.float32)
    bias = np.zeros((1, 2 * LANES), np.float32)
    lane_rows = np.zeros((2 * HEADS, LANES), np.float32)
    for hd in range(HEADS):
        lane0 = HEAD_DIM + hd * DECAY_LANES
        lane_rows[hd, lane0:lane0 + DECAY_LANES] = 1.0
        lane_rows[HEADS, lane0 + N_SPLIT] = PAD_KEY_BIAS
        for j in range(N_SPLIT):
            e[j * LANES + hd, lane0 + j] = 1.0
            bias[0, lane0 + N_SPLIT + j] = 1.0
            bias[0, LANES + lane0 + j] = 1.0
            e[j * LANES + hd, LANES + lane0 + N_SPLIT + j] = -1.0
    return jnp.asarray(e, jnp.bfloat16), jnp.asarray(bias, jnp.float32), jnp.asarray(lane_rows, jnp.float32)


def _pack_w_in(w):
    q = w[:, 0:D_ATTN] * (HEAD_DIM ** -0.5)
    k = w[:, D_ATTN:2 * D_ATTN]
    v = w[:, 2 * D_ATTN:3 * D_ATTN]
    f = jnp.pad(w[:, 3 * D_ATTN:3 * D_ATTN + HEADS], ((0, 0), (0, LANES - HEADS)))
    rest = w[:, 3 * D_ATTN + HEADS:]
    return jnp.concatenate([f, q, k, v, rest], -1).astype(jnp.bfloat16)


def kernel(x, meta_tokens, mix_norm_g, w_in, b_forget, w_conf_dw, b_conf_dw, conf_ln_g, conf_ln_b,
           w_conf_pw, b_conf_pw, w_sc_conv, w_out, mlp_norm_g, w_mlp1, w_mlp2, final_norm_g):
    bsz, seq, d = x.shape
    depth = w_in.shape[0]
    assert seq % ROW_TILE == 0 and meta_tokens.shape[0] == N_META and PAD_FRONT >= BK and BQ == 2 * BK
    l_pad = ROW_TILE + seq

    meta_tile = jnp.concatenate([jnp.zeros((PAD_FRONT, d), x.dtype), meta_tokens.astype(x.dtype)], axis=0)
    tri = jnp.asarray(np.tril(np.ones((ROW_TILE, ROW_TILE), np.float32)), jnp.bfloat16)
    e_mat, qk_bias, lane_rows = _decay_selectors()
    row = lambda a: a.reshape(1, -1).astype(jnp.float32)
    fg = row(final_norm_g)

    h = x
    for l in range(depth):
        first, final = l == 0, l == depth - 1
        bf_row = jnp.pad(row(b_forget[l]), ((0, 0), (0, LANES - HEADS)))
        qp, kp, vt, mixin = _proj_call(h, meta_tile, row(mix_norm_g[l]), _pack_w_in(w_in[l]), bf_row, tri, e_mat,
                                       qk_bias, lane_rows, first=first, l_pad=l_pad)
        ot = _attn_call(qp, kp, vt).reshape(bsz, D_ATTN, l_pad)
        wdw = jnp.pad(w_conf_dw[l], ((0, HALO - CONF_K), (0, 0)))
        wsc = jnp.pad(w_sc_conv[l], ((0, 8 - SC_K), (0, 0)))
        h = _mix_call(h, meta_tile, ot, mixin, wdw, row(b_conf_dw[l]), row(conf_ln_g[l]), row(conf_ln_b[l]),
                      w_conf_pw[l].astype(jnp.bfloat16), row(b_conf_pw[l]), wsc,
                      w_out[l].astype(jnp.bfloat16), row(mlp_norm_g[l]),
                      w_mlp1[l].astype(jnp.bfloat16), w_mlp2[l].astype(jnp.bfloat16), fg,
                      first=first, final=final, l_pad=l_pad)
    return h
```

```python
import functools

import jax
import jax.numpy as jnp
import numpy as np
from jax import lax
from jax.experimental import pallas as pl
from jax.experimental.pallas import tpu as pltpu

D_MODEL = 1024
N_META = 16
HEADS = 8
HEAD_DIM = 64
D_ATTN = HEADS * HEAD_DIM
D_CONF = 256
D_SC = 256
CONF_K = 31
SC_K = 3
D_FF = 4 * D_MODEL
EPS = 1e-6

LANES = 128
SUBLANES = 8
ROW_TILE = 512
PAD_FRONT = ROW_TILE - N_META
BQ = ROW_TILE
BK = 256
HEADS_PER_STEP = 2
LOG2E = 1.4426950408889634
PAD_KEY_BIAS = -2.0 ** 100
HALO = 32
SHIFT_ROWS = ROW_TILE + HALO - SUBLANES
FF_CHUNK = 512
CONV_PARTS = 7
N_SPLIT = 3
VMEM_LIMIT = 56 * 1024 * 1024

C_F = 0
C_Q = C_F + LANES
C_K = C_Q + D_ATTN
C_V = C_K + D_ATTN
C_REST = C_V + D_ATTN
N_PACK = C_REST + 2 * D_CONF + 3 * D_SC
DECAY_LANES = 2 * N_SPLIT


def _split_bf16(x):
    pieces = []
    r = x
    for _ in range(N_SPLIT):
        p = r.astype(jnp.bfloat16)
        pieces.append(p)
        r = r - p.astype(jnp.float32)
    return pieces


def _rms(x, g):
    return x * lax.rsqrt(jnp.mean(x * x, axis=-1, keepdims=True) + EPS) * g


def _tile_rows(src_ref, meta_ref, first):
    h = src_ref[0]
    if first:
        h = jnp.where(pl.program_id(1) == 0, meta_ref[...], h)
    return h


def _pad_rows_below(shape):
    limit = jnp.where(pl.program_id(1) == 0, PAD_FRONT, 0)
    return lax.broadcasted_iota(jnp.int32, shape, 0) < limit


def _proj_kernel(src_ref, meta_ref, g_ref, w_ref, bf_ref, tri_ref, e_ref, qkb_ref, lanes_ref,
                 qp_ref, kp_ref, vt_ref, mix_ref, carry_ref, *, first):
    @pl.when(pl.program_id(1) == 0)
    def _():
        carry_ref[...] = jnp.zeros_like(carry_ref)

    hn = _rms(_tile_rows(src_ref, meta_ref, first), g_ref[...]).astype(jnp.bfloat16)

    z = jnp.dot(hn, w_ref[:, C_F:C_F + LANES], preferred_element_type=jnp.float32) + bf_ref[...]
    log_f = jnp.minimum(z, 0.0) - jnp.log1p(jnp.exp(-jnp.abs(z)))
    tri = tri_ref[...]
    c = carry_ref[0:1, :]
    for piece in _split_bf16(log_f):
        c = c + jnp.dot(tri, piece, preferred_element_type=jnp.float32)
    carry_ref[0:1, :] = c[ROW_TILE - 1:ROW_TILE, :]

    c_pieces = jnp.concatenate(_split_bf16(c * LOG2E), axis=-1)
    decay = jnp.dot(c_pieces, e_ref[...], preferred_element_type=jnp.float32) + qkb_ref[...]
    decay_q, decay_k = decay[:, :LANES], decay[:, LANES:]
    decay_k = decay_k + jnp.where(_pad_rows_below((ROW_TILE, LANES)), lanes_ref[HEADS:HEADS + 1, :], 0.0)
    q = jnp.dot(hn, w_ref[:, C_Q:C_K], preferred_element_type=jnp.float32) * LOG2E
    k = jnp.dot(hn, w_ref[:, C_K:C_V], preferred_element_type=jnp.float32)
    low_half = lax.broadcasted_iota(jnp.int32, (ROW_TILE, LANES), 1) < HEAD_DIM
    for hd in range(HEADS):
        pair_cols = slice((hd // 2) * LANES, (hd // 2 + 1) * LANES)
        q_h, k_h = q[:, pair_cols], k[:, pair_cols]
        if hd % 2:
            q_h, k_h = pltpu.roll(q_h, HEAD_DIM, axis=1), pltpu.roll(k_h, HEAD_DIM, axis=1)
        qp_ref[0, hd] = jnp.where(low_half, q_h, decay_q).astype(jnp.bfloat16)
        kp_ref[0, hd] = jnp.where(low_half, k_h, decay_k * lanes_ref[hd:hd + 1, :]).astype(jnp.bfloat16)

    v = jnp.dot(hn, w_ref[:, C_V:C_REST], preferred_element_type=jnp.float32)
    vt = v.T.astype(jnp.bfloat16)
    for hd in range(HEADS):
        vt_ref[0, hd] = vt[hd * HEAD_DIM:(hd + 1) * HEAD_DIM, :]

    r = jnp.dot(hn, w_ref[:, C_REST:N_PACK], preferred_element_type=jnp.float32)
    conf_a = r[:, 0:D_CONF]
    conf_g = r[:, D_CONF:2 * D_CONF]
    sc_b = r[:, 2 * D_CONF:2 * D_CONF + D_SC]
    sc_c = r[:, 2 * D_CONF + D_SC:2 * D_CONF + 2 * D_SC]
    sc_u = r[:, 2 * D_CONF + 2 * D_SC:]
    mix_ref[0, :, 0:D_CONF] = conf_a * jax.nn.sigmoid(conf_g)
    mix_ref[0, :, D_CONF:D_CONF + D_SC] = sc_b
    mix_ref[0, :, D_CONF + D_SC:] = sc_c * sc_u


def _src_spec(first):
    if first:
        return pl.BlockSpec((1, ROW_TILE, D_MODEL), lambda b, i: (b, jnp.maximum(i - 1, 0), 0))
    return pl.BlockSpec((1, ROW_TILE, D_MODEL), lambda b, i: (b, i, 0))


def _proj_call(src, meta_tile, g, w_pack, bf_row, tri, e_mat, qk_bias, lane_rows, *, first, l_pad):
    bsz = src.shape[0]
    nt = l_pad // ROW_TILE
    const = lambda b, i: (0, 0)
    return pl.pallas_call(
        functools.partial(_proj_kernel, first=first),
        grid=(bsz, nt),
        in_specs=[
            _src_spec(first),
            pl.BlockSpec((ROW_TILE, D_MODEL), const),
            pl.BlockSpec((1, D_MODEL), const),
            pl.BlockSpec((D_MODEL, N_PACK), const),
            pl.BlockSpec((1, LANES), const),
            pl.BlockSpec((ROW_TILE, ROW_TILE), const),
            pl.BlockSpec((N_SPLIT * LANES, 2 * LANES), const),
            pl.BlockSpec((1, 2 * LANES), const),
            pl.BlockSpec((2 * HEADS, LANES), const),
        ],
        out_specs=[
            pl.BlockSpec((1, HEADS, ROW_TILE, LANES), lambda b, i: (b, 0, i, 0)),
            pl.BlockSpec((1, HEADS, ROW_TILE, LANES), lambda b, i: (b, 0, i, 0)),
            pl.BlockSpec((1, HEADS, HEAD_DIM, ROW_TILE), lambda b, i: (b, 0, 0, i)),
            pl.BlockSpec((1, ROW_TILE, D_CONF + 2 * D_SC), lambda b, i: (b, i, 0)),
        ],
        out_shape=[
            jax.ShapeDtypeStruct((bsz, HEADS, l_pad, LANES), jnp.bfloat16),
            jax.ShapeDtypeStruct((bsz, HEADS, l_pad, LANES), jnp.bfloat16),
            jax.ShapeDtypeStruct((bsz, HEADS, HEAD_DIM, l_pad), jnp.bfloat16),
            jax.ShapeDtypeStruct((bsz, l_pad, D_CONF + 2 * D_SC), jnp.float32),
        ],
        scratch_shapes=[pltpu.VMEM((8, LANES), jnp.float32)],
        compiler_params=pltpu.CompilerParams(
            dimension_semantics=("arbitrary", "arbitrary"), vmem_limit_bytes=VMEM_LIMIT),
        name="proj_first" if first else "proj",
    )(src, meta_tile, g, w_pack, bf_row, tri, e_mat, qk_bias, lane_rows)


def _attn_kernel(qp_ref, kp_ref, vt_ref, o_ref, sa_ref, sb_ref):
    qi = pl.program_id(2)
    heads = range(HEADS_PER_STEP)
    qs = [qp_ref[0, hd] for hd in heads]

    def scores(j, s_ref):
        k0 = pl.multiple_of(j * BK, BK)
        for hd in heads:
            s_ref[hd] = lax.dot_general(kp_ref[0, hd, pl.ds(k0, BK), :], qs[hd], (((1,), (1,)), ((), ())),
                                        preferred_element_type=jnp.float32)

    def consume(j, s_ref, carries, diag):
        k0 = pl.multiple_of(j * BK, BK)
        out = []
        for hd in heads:
            m, l, acc = carries[hd]
            s = s_ref[hd]
            if diag is not None:
                kpos = lax.broadcasted_iota(jnp.int32, (BK, BQ), 0) + diag * BK
                qpos = lax.broadcasted_iota(jnp.int32, (BK, BQ), 1)
                s = jnp.where(kpos <= qpos, s, -jnp.inf)
            m_new = jnp.maximum(m, jnp.max(s, axis=0, keepdims=True))
            alpha = jnp.exp2(m - m_new)
            p = jnp.exp2(s - m_new)
            l = alpha * l + jnp.sum(p, axis=0, keepdims=True)
            vblk = vt_ref[0, hd, :, pl.ds(k0, BK)]
            acc = alpha * acc + jnp.dot(vblk, p.astype(jnp.bfloat16),
                                        preferred_element_type=jnp.float32)
            out.append((m_new, l, acc))
        return tuple(out)

    def finish(carries):
        for hd, (m, l, acc) in enumerate(carries):
            o_ref[0, hd] = (acc / l).astype(o_ref.dtype)

    init = ((jnp.full((1, BQ), -jnp.inf, jnp.float32), jnp.zeros((1, BQ), jnp.float32),
             jnp.zeros((HEAD_DIM, BQ), jnp.float32)),) * HEADS_PER_STEP

    @pl.when(qi == 0)
    def _():
        scores(0, sa_ref)
        scores(1, sb_ref)
        finish(consume(1, sb_ref, consume(0, sa_ref, init, 0), 1))

    @pl.when(qi > 0)
    def _():
        def pair(t, carries):
            j = 2 * t + 1
            scores(j + 1, sb_ref)
            carries = consume(j, sa_ref, carries, None)
            scores(j + 2, sa_ref)
            return consume(j + 1, sb_ref, carries, None)

        scores(1, sa_ref)
        carries = lax.fori_loop(0, qi - 1, pair, init)
        last = 2 * qi - 1
        scores(last + 1, sb_ref)
        carries = consume(last, sa_ref, carries, None)
        scores(last + 2, sa_ref)
        carries = consume(last + 1, sb_ref, carries, 0)
        finish(consume(last + 2, sa_ref, carries, 1))


def _attn_call(qp, kp, vt):
    bsz, heads, l_pad, _ = qp.shape
    nq = l_pad // BQ
    hps = HEADS_PER_STEP
    return pl.pallas_call(
        _attn_kernel,
        grid=(bsz, heads // hps, nq),
        in_specs=[
            pl.BlockSpec((1, hps, BQ, LANES), lambda b, h, i: (b, h, i, 0)),
            pl.BlockSpec((1, hps, l_pad, LANES), lambda b, h, i: (b, h, 0, 0)),
            pl.BlockSpec((1, hps, HEAD_DIM, l_pad), lambda b, h, i: (b, h, 0, 0)),
        ],
        out_specs=pl.BlockSpec((1, hps, HEAD_DIM, BQ), lambda b, h, i: (b, h, 0, i)),
        out_shape=jax.ShapeDtypeStruct((bsz, heads, HEAD_DIM, l_pad), jnp.bfloat16),
        scratch_shapes=[pltpu.VMEM((hps, BK, BQ), jnp.float32), pltpu.VMEM((hps, BK, BQ), jnp.float32)],
        compiler_params=pltpu.CompilerParams(
            dimension_semantics=("arbitrary", "arbitrary", "arbitrary"), vmem_limit_bytes=VMEM_LIMIT),
        name="attn",
    )(qp, kp, vt)


def _causal_taps(ext_ref, w_ref, n_taps, k_lo, k_hi, shift_ref=None):
    out = None
    for k in range(k_lo, k_hi):
        start = HALO - (n_taps - 1) + k
        if shift_ref is None or start % SUBLANES == 0:
            window = ext_ref[pl.ds(start, ROW_TILE), :]
        else:
            window = shift_ref[start % SUBLANES - 1, pl.ds(start - start % SUBLANES, ROW_TILE), :]
        term = window * w_ref[k:k + 1, :]
        out = term if out is None else out + term
    return out


def _zero_of(x):
    bits = pltpu.bitcast(x, jnp.uint32)
    return lax.shift_right_logical(lax.shift_right_logical(bits, jnp.uint32(16)), jnp.uint32(16)).astype(jnp.float32)


def _conv_parts(mixt_ref, halo, wdw_ref, bdw_ref, lng_ref, lnb_ref, wsc_ref, ext_a, ext_c, shift_a, y_ref, sc_ref):
    tap_parts = CONV_PARTS - 2
    bounds = [round(i * CONF_K / tap_parts) for i in range(tap_parts + 1)]
    state = {}

    def first_part(edge):
        ext_a[0:HALO, :] = halo[:, 0:D_CONF] + edge
        ext_a[HALO:, :] = mixt_ref[0, :, 0:D_CONF]
        ext_c[0:HALO, :] = halo[:, D_CONF + D_SC:] + edge
        ext_c[HALO:, :] = mixt_ref[0, :, D_CONF + D_SC:]
        for r in range(1, SUBLANES):
            shift_a[r - 1] = ext_a[pl.ds(r, SHIFT_ROWS), :]
        state["dw"] = bdw_ref[...]

    def middle_part(part, edge):
        state["dw"] = state["dw"] + edge + _causal_taps(ext_a, wdw_ref, CONF_K, bounds[part - 1], bounds[part],
                                                        shift_a)

    def last_part(edge):
        dw = state["dw"] + edge
        mu = jnp.mean(dw, axis=-1, keepdims=True)
        xc = dw - mu
        y = xc * lax.rsqrt(jnp.mean(xc * xc, axis=-1, keepdims=True) + EPS) * lng_ref[...] + lnb_ref[...]
        y = y * jax.nn.sigmoid(y)
        sc = mixt_ref[0, :, D_CONF:D_CONF + D_SC] * _causal_taps(ext_c, wsc_ref, SC_K, 0, SC_K)
        y_ref[...] = y.astype(y_ref.dtype)
        sc_ref[...] = sc.astype(sc_ref.dtype)
        return _zero_of(jnp.max(y, axis=0, keepdims=True) + jnp.max(sc, axis=0, keepdims=True))

    middle = [functools.partial(middle_part, part) for part in range(1, CONV_PARTS - 1)]
    return [first_part] + middle + [last_part]


def _mix_kernel(src_ref, meta_ref, ot_ref, mix_ref, mixn_ref, wdw_ref, bdw_ref, lng_ref, lnb_ref, wpw_ref,
                bpw_ref, wsc_ref, wout_ref, ng_ref, w1_ref, w2_ref, fg_ref, out_ref,
                ext_a, ext_c, shift_a, y_ref, sc_ref, *, first, final):
    conv_parts = functools.partial(_conv_parts, wdw_ref=wdw_ref, bdw_ref=bdw_ref, lng_ref=lng_ref, lnb_ref=lnb_ref,
                                   wsc_ref=wsc_ref, ext_a=ext_a, ext_c=ext_c, shift_a=shift_a,
                                   y_ref=y_ref, sc_ref=sc_ref)

    @pl.when(pl.program_id(1) == 0)
    def _():
        for part in conv_parts(mix_ref, jnp.zeros((HALO, D_CONF + 2 * D_SC), jnp.float32)):
            part(0.0)

    conf = jnp.dot(y_ref[...], wpw_ref[...], preferred_element_type=jnp.float32) + bpw_ref[...]
    sc = sc_ref[...]

    attn = ot_ref[0].T
    mixed = (jnp.dot(attn, wout_ref[0:D_ATTN, :], preferred_element_type=jnp.float32)
             + jnp.dot(conf.astype(jnp.bfloat16), wout_ref[D_ATTN:D_ATTN + D_CONF, :],
                       preferred_element_type=jnp.float32)
             + jnp.dot(sc, wout_ref[D_ATTN + D_CONF:, :], preferred_element_type=jnp.float32))
    h1 = _tile_rows(src_ref, meta_ref, first) + mixed

    next_parts = conv_parts(mixn_ref, mix_ref[0, ROW_TILE - HALO:, :])
    hn = _rms(h1, ng_ref[...]).astype(jnp.bfloat16)
    acc = h1
    for chunk, c0 in enumerate(range(0, D_FF, FF_CHUNK)):
        if chunk == CONV_PARTS:
            hn = hn + jnp.concatenate([conv_zero] * (D_MODEL // D_CONF), axis=1).astype(jnp.bfloat16)
        a = jnp.dot(hn, w1_ref[:, c0:c0 + FF_CHUNK], preferred_element_type=jnp.float32)
        if chunk < CONV_PARTS:
            conv_zero = next_parts[chunk](_zero_of(a[0:1, 0:D_CONF]))
        a = jnp.maximum(a, 0.0)
        acc = acc + jnp.dot((a * a).astype(jnp.bfloat16), w2_ref[c0:c0 + FF_CHUNK, :],
                            preferred_element_type=jnp.float32)
    if final:
        out_ref[0] = _rms(acc, fg_ref[...])
    else:
        out_ref[0] = jnp.where(_pad_rows_below(acc.shape), 0.0, acc)


def _mix_call(src, meta_tile, ot, mixin, wdw, bdw, lng, lnb, wpw, bpw, wsc, wout, ng, w1, w2, fg,
              *, first, final, l_pad):
    bsz = src.shape[0]
    nt = l_pad // ROW_TILE
    const = lambda b, i: (0, 0)
    once = pl.Buffered(1)
    wide = D_CONF + 2 * D_SC
    if final:
        out_spec = pl.BlockSpec((1, ROW_TILE, D_MODEL), lambda b, i: (b, jnp.maximum(i - 1, 0), 0))
        out_shape = jax.ShapeDtypeStruct((bsz, l_pad - ROW_TILE, D_MODEL), jnp.float32)
    else:
        out_spec = pl.BlockSpec((1, ROW_TILE, D_MODEL), lambda b, i: (b, i, 0))
        out_shape = jax.ShapeDtypeStruct((bsz, l_pad, D_MODEL), jnp.float32)
    return pl.pallas_call(
        functools.partial(_mix_kernel, first=first, final=final),
        grid=(bsz, nt),
        in_specs=[
            _src_spec(first),
            pl.BlockSpec((ROW_TILE, D_MODEL), const),
            pl.BlockSpec((1, D_ATTN, ROW_TILE), lambda b, i: (b, 0, i)),
            pl.BlockSpec((1, ROW_TILE, wide), lambda b, i: (b, i, 0)),
            pl.BlockSpec((1, ROW_TILE, wide), lambda b, i: (b, jnp.minimum(i + 1, nt - 1), 0)),
            pl.BlockSpec((HALO, D_CONF), const),
            pl.BlockSpec((1, D_CONF), const),
            pl.BlockSpec((1, D_CONF), const),
            pl.BlockSpec((1, D_CONF), const),
            pl.BlockSpec((D_CONF, D_CONF), const),
            pl.BlockSpec((1, D_CONF), const),
            pl.BlockSpec((8, D_SC), const),
            pl.BlockSpec((D_MODEL, D_MODEL), const, pipeline_mode=once),
            pl.BlockSpec((1, D_MODEL), const),
            pl.BlockSpec((D_MODEL, D_FF), const, pipeline_mode=once),
            pl.BlockSpec((D_FF, D_MODEL), const, pipeline_mode=once),
            pl.BlockSpec((1, D_MODEL), const),
        ],
        out_specs=out_spec,
        out_shape=out_shape,
        scratch_shapes=[pltpu.VMEM((HALO + ROW_TILE, D_CONF), jnp.float32),
                        pltpu.VMEM((HALO + ROW_TILE, D_SC), jnp.float32),
                        pltpu.VMEM((SUBLANES - 1, SHIFT_ROWS, D_CONF), jnp.float32),
                        pltpu.VMEM((ROW_TILE, D_CONF), jnp.bfloat16),
                        pltpu.VMEM((ROW_TILE, D_SC), jnp.bfloat16)],
        compiler_params=pltpu.CompilerParams(
            dimension_semantics=("arbitrary", "arbitrary"), vmem_limit_bytes=VMEM_LIMIT),
        name="mix_final" if final else "mix",
    )(src, meta_tile, ot, mixin, mixin, wdw, bdw, lng, lnb, wpw, bpw, wsc, wout, ng, w1, w2, fg)


def _decay_selectors():
    assert HEAD_DIM + HEADS * DECAY_LANES <= LANES
    e = np.zeros((N_SPLIT * LANES, 2 * LANES), np.float32)
    bias = np.zeros((1, 2 * LANES), np.float32)
    lane_rows = np.zeros((2 * HEADS, LANES), np.float32)
    for hd in range(HEADS):
        lane0 = HEAD_DIM + hd * DECAY_LANES
        lane_rows[hd, lane0:lane0 + DECAY_LANES] = 1.0
        lane_rows[HEADS, lane0 + N_SPLIT] = PAD_KEY_BIAS
        for j in range(N_SPLIT):
            e[j * LANES + hd, lane0 + j] = 1.0
            bias[0, lane0 + N_SPLIT + j] = 1.0
            bias[0, LANES + lane0 + j] = 1.0
            e[j * LANES + hd, LANES + lane0 + N_SPLIT + j] = -1.0
    return jnp.asarray(e, jnp.bfloat16), jnp.asarray(bias, jnp.float32), jnp.asarray(lane_rows, jnp.float32)


def _pack_w_in(w):
    q = w[:, 0:D_ATTN] * (HEAD_DIM ** -0.5)
    k = w[:, D_ATTN:2 * D_ATTN]
    v = w[:, 2 * D_ATTN:3 * D_ATTN]
    f = jnp.pad(w[:, 3 * D_ATTN:3 * D_ATTN + HEADS], ((0, 0), (0, LANES - HEADS)))
    rest = w[:, 3 * D_ATTN + HEADS:]
    return jnp.concatenate([f, q, k, v, rest], -1).astype(jnp.bfloat16)


def kernel(x, meta_tokens, mix_norm_g, w_in, b_forget, w_conf_dw, b_conf_dw, conf_ln_g, conf_ln_b,
           w_conf_pw, b_conf_pw, w_sc_conv, w_out, mlp_norm_g, w_mlp1, w_mlp2, final_norm_g):
    bsz, seq, d = x.shape
    depth = w_in.shape[0]
    assert seq % ROW_TILE == 0 and meta_tokens.shape[0] == N_META and PAD_FRONT >= BK and BQ == 2 * BK
    l_pad = ROW_TILE + seq

    meta_tile = jnp.concatenate([jnp.zeros((PAD_FRONT, d), x.dtype), meta_tokens.astype(x.dtype)], axis=0)
    tri = jnp.asarray(np.tril(np.ones((ROW_TILE, ROW_TILE), np.float32)), jnp.bfloat16)
    e_mat, qk_bias, lane_rows = _decay_selectors()
    row = lambda a: a.reshape(1, -1).astype(jnp.float32)
    fg = row(final_norm_g)

    h = x
    for l in range(depth):
        first, final = l == 0, l == depth - 1
        bf_row = jnp.pad(row(b_forget[l]), ((0, 0), (0, LANES - HEADS)))
        qp, kp, vt, mixin = _proj_call(h, meta_tile, row(mix_norm_g[l]), _pack_w_in(w_in[l]), bf_row, tri, e_mat,
                                       qk_bias, lane_rows, first=first, l_pad=l_pad)
        ot = _attn_call(qp, kp, vt).reshape(bsz, D_ATTN, l_pad)
        wdw = jnp.pad(w_conf_dw[l], ((0, HALO - CONF_K), (0, 0)))
        wsc = jnp.pad(w_sc_conv[l], ((0, 8 - SC_K), (0, 0)))
        h = _mix_call(h, meta_tile, ot, mixin, wdw, row(b_conf_dw[l]), row(conf_ln_g[l]), row(conf_ln_b[l]),
                      w_conf_pw[l].astype(jnp.bfloat16), row(b_conf_pw[l]), wsc,
                      w_out[l].astype(jnp.bfloat16), row(mlp_norm_g[l]),
                      w_mlp1[l].astype(jnp.bfloat16), w_mlp2[l].astype(jnp.bfloat16), fg,
                      first=first, final=final, l_pad=l_pad)
    return h
```

```python
import functools

import jax
import jax.numpy as jnp
import numpy as np
from jax import lax
from jax.experimental import pallas as pl
from jax.experimental.pallas import tpu as pltpu

D_MODEL = 1024
N_META = 16
HEADS = 8
HEAD_DIM = 64
D_ATTN = HEADS * HEAD_DIM
D_CONF = 256
D_SC = 256
CONF_K = 31
SC_K = 3
D_FF = 4 * D_MODEL
EPS = 1e-6

LANES = 128
SUBLANES = 8
ROW_TILE = 512
PAD_FRONT = ROW_TILE - N_META
BQ = ROW_TILE
BK = 256
HEADS_PER_STEP = 2
LOG2E = 1.4426950408889634
PAD_KEY_BIAS = -2.0 ** 100
HALO = 32
SHIFT_ROWS = ROW_TILE + HALO - SUBLANES
FF_CHUNK = 512
CONV_PARTS = 7
N_SPLIT = 3
VMEM_LIMIT = 56 * 1024 * 1024

C_F = 0
C_Q = C_F + LANES
C_K = C_Q + D_ATTN
C_V = C_K + D_ATTN
C_REST = C_V + D_ATTN
N_PACK = C_REST + 2 * D_CONF + 3 * D_SC
DECAY_LANES = 2 * N_SPLIT


def _split_bf16(x):
    pieces = []
    r = x
    for _ in range(N_SPLIT):
        p = r.astype(jnp.bfloat16)
        pieces.append(p)
        r = r - p.astype(jnp.float32)
    return pieces


def _rms(x, g):
    return x * lax.rsqrt(jnp.mean(x * x, axis=-1, keepdims=True) + EPS) * g


def _tile_rows(src_ref, meta_ref, first):
    h = src_ref[0]
    if first:
        h = jnp.where(pl.program_id(1) == 0, meta_ref[...], h)
    return h


def _pad_rows_below(shape):
    limit = jnp.where(pl.program_id(1) == 0, PAD_FRONT, 0)
    return lax.broadcasted_iota(jnp.int32, shape, 0) < limit


def _proj_kernel(src_ref, meta_ref, g_ref, w_ref, bf_ref, tri_ref, e_ref, qkb_ref, lanes_ref,
                 qp_ref, kp_ref, vt_ref, mix_ref, carry_ref, *, first):
    @pl.when(pl.program_id(1) == 0)
    def _():
        carry_ref[...] = jnp.zeros_like(carry_ref)

    hn = _rms(_tile_rows(src_ref, meta_ref, first), g_ref[...]).astype(jnp.bfloat16)

    fq = jnp.dot(hn, w_ref[:, C_F:C_K], preferred_element_type=jnp.float32)
    z = fq[:, :LANES] + bf_ref[...]
    log_f = jnp.minimum(z, 0.0) - jnp.log1p(jnp.exp(-jnp.abs(z)))
    sums = jnp.dot(tri_ref[...], jnp.concatenate(_split_bf16(log_f), axis=-1),
                   preferred_element_type=jnp.float32)
    c = carry_ref[0:1, :]
    for piece in range(N_SPLIT):
        c = c + sums[:, piece * LANES:(piece + 1) * LANES]
    carry_ref[0:1, :] = c[ROW_TILE - 1:ROW_TILE, :]

    c_pieces = jnp.concatenate(_split_bf16(c * LOG2E), axis=-1)
    decay = jnp.dot(c_pieces, e_ref[...], preferred_element_type=jnp.float32) + qkb_ref[...]
    decay_q, decay_k = decay[:, :LANES], decay[:, LANES:]
    decay_k = decay_k + jnp.where(_pad_rows_below((ROW_TILE, LANES)), lanes_ref[HEADS:HEADS + 1, :], 0.0)
    q = fq[:, LANES:] * LOG2E
    k = jnp.dot(hn, w_ref[:, C_K:C_V], preferred_element_type=jnp.float32)
    low_half = lax.broadcasted_iota(jnp.int32, (ROW_TILE, LANES), 1) < HEAD_DIM
    for hd in range(HEADS):
        pair_cols = slice((hd // 2) * LANES, (hd // 2 + 1) * LANES)
        q_h, k_h = q[:, pair_cols], k[:, pair_cols]
        if hd % 2:
            q_h, k_h = pltpu.roll(q_h, HEAD_DIM, axis=1), pltpu.roll(k_h, HEAD_DIM, axis=1)
        qp_ref[0, hd] = jnp.where(low_half, q_h, decay_q).astype(jnp.bfloat16)
        kp_ref[0, hd] = jnp.where(low_half, k_h, decay_k * lanes_ref[hd:hd + 1, :]).astype(jnp.bfloat16)

    v = jnp.dot(hn, w_ref[:, C_V:C_REST], preferred_element_type=jnp.float32)
    vt = v.T.astype(jnp.bfloat16)
    for hd in range(HEADS):
        vt_ref[0, hd] = vt[hd * HEAD_DIM:(hd + 1) * HEAD_DIM, :]

    r = jnp.dot(hn, w_ref[:, C_REST:N_PACK], preferred_element_type=jnp.float32)
    conf_a = r[:, 0:D_CONF]
    conf_g = r[:, D_CONF:2 * D_CONF]
    sc_b = r[:, 2 * D_CONF:2 * D_CONF + D_SC]
    sc_c = r[:, 2 * D_CONF + D_SC:2 * D_CONF + 2 * D_SC]
    sc_u = r[:, 2 * D_CONF + 2 * D_SC:]
    mix_ref[0, :, 0:D_CONF] = conf_a * jax.nn.sigmoid(conf_g)
    mix_ref[0, :, D_CONF:D_CONF + D_SC] = sc_b
    mix_ref[0, :, D_CONF + D_SC:] = sc_c * sc_u


def _src_spec(first):
    if first:
        return pl.BlockSpec((1, ROW_TILE, D_MODEL), lambda b, i: (b, jnp.maximum(i - 1, 0), 0))
    return pl.BlockSpec((1, ROW_TILE, D_MODEL), lambda b, i: (b, i, 0))


def _proj_call(src, meta_tile, g, w_pack, bf_row, tri, e_mat, qk_bias, lane_rows, *, first, l_pad):
    bsz = src.shape[0]
    nt = l_pad // ROW_TILE
    const = lambda b, i: (0, 0)
    return pl.pallas_call(
        functools.partial(_proj_kernel, first=first),
        grid=(bsz, nt),
        in_specs=[
            _src_spec(first),
            pl.BlockSpec((ROW_TILE, D_MODEL), const),
            pl.BlockSpec((1, D_MODEL), const),
            pl.BlockSpec((D_MODEL, N_PACK), const),
            pl.BlockSpec((1, LANES), const),
            pl.BlockSpec((ROW_TILE, ROW_TILE), const),
            pl.BlockSpec((N_SPLIT * LANES, 2 * LANES), const),
            pl.BlockSpec((1, 2 * LANES), const),
            pl.BlockSpec((2 * HEADS, LANES), const),
        ],
        out_specs=[
            pl.BlockSpec((1, HEADS, ROW_TILE, LANES), lambda b, i: (b, 0, i, 0)),
            pl.BlockSpec((1, HEADS, ROW_TILE, LANES), lambda b, i: (b, 0, i, 0)),
            pl.BlockSpec((1, HEADS, HEAD_DIM, ROW_TILE), lambda b, i: (b, 0, 0, i)),
            pl.BlockSpec((1, ROW_TILE, D_CONF + 2 * D_SC), lambda b, i: (b, i, 0)),
        ],
        out_shape=[
            jax.ShapeDtypeStruct((bsz, HEADS, l_pad, LANES), jnp.bfloat16),
            jax.ShapeDtypeStruct((bsz, HEADS, l_pad, LANES), jnp.bfloat16),
            jax.ShapeDtypeStruct((bsz, HEADS, HEAD_DIM, l_pad), jnp.bfloat16),
            jax.ShapeDtypeStruct((bsz, l_pad, D_CONF + 2 * D_SC), jnp.float32),
        ],
        scratch_shapes=[pltpu.VMEM((8, LANES), jnp.float32)],
        compiler_params=pltpu.CompilerParams(
            dimension_semantics=("arbitrary", "arbitrary"), vmem_limit_bytes=VMEM_LIMIT),
        name="proj_first" if first else "proj",
    )(src, meta_tile, g, w_pack, bf_row, tri, e_mat, qk_bias, lane_rows)


def _attn_kernel(qp_ref, kp_ref, vt_ref, o_ref, sa_ref, sb_ref):
    qi = pl.program_id(2)
    heads = range(HEADS_PER_STEP)
    qs = [qp_ref[0, hd] for hd in heads]

    def scores(j, s_ref, diag=None):
        k0 = pl.multiple_of(j * BK, BK)
        q0 = 0 if diag is None else diag * BK
        for hd in heads:
            s_ref[hd, :, q0:] = lax.dot_general(kp_ref[0, hd, pl.ds(k0, BK), :], qs[hd][q0:],
                                                (((1,), (1,)), ((), ())),
                                                preferred_element_type=jnp.float32)

    def consume(j, s_ref, carries, diag):
        k0 = pl.multiple_of(j * BK, BK)
        q0 = 0 if diag is None else diag * BK
        out = []
        for hd in heads:
            m, l, acc = (c[:, q0:] for c in carries[hd])
            s = s_ref[hd, :, q0:]
            if diag is not None:
                kpos = lax.broadcasted_iota(jnp.int32, s.shape, 0) + diag * BK
                qpos = lax.broadcasted_iota(jnp.int32, s.shape, 1) + q0
                s = jnp.where(kpos <= qpos, s, -jnp.inf)
            m_new = jnp.maximum(m, jnp.max(s, axis=0, keepdims=True))
            alpha = jnp.exp2(m - m_new)
            p = jnp.exp2(s - m_new)
            l = alpha * l + jnp.sum(p, axis=0, keepdims=True)
            vblk = vt_ref[0, hd, :, pl.ds(k0, BK)]
            acc = alpha * acc + jnp.dot(vblk, p.astype(jnp.bfloat16),
                                        preferred_element_type=jnp.float32)
            new = (m_new, l, acc)
            if q0:
                new = tuple(jnp.concatenate([c[:, :q0], n], axis=1) for c, n in zip(carries[hd], new))
            out.append(new)
        return tuple(out)

    def finish(carries):
        for hd, (m, l, acc) in enumerate(carries):
            o_ref[0, hd] = (acc / l).astype(o_ref.dtype)

    init = ((jnp.full((1, BQ), -jnp.inf, jnp.float32), jnp.zeros((1, BQ), jnp.float32),
             jnp.zeros((HEAD_DIM, BQ), jnp.float32)),) * HEADS_PER_STEP

    @pl.when(qi == 0)
    def _():
        scores(0, sa_ref, 0)
        scores(1, sb_ref, 1)
        finish(consume(1, sb_ref, consume(0, sa_ref, init, 0), 1))

    @pl.when(qi > 0)
    def _():
        def pair(t, carries):
            j = 2 * t + 1
            scores(j + 1, sb_ref)
            carries = consume(j, sa_ref, carries, None)
            scores(j + 2, sa_ref)
            return consume(j + 1, sb_ref, carries, None)

        scores(1, sa_ref)
        carries = lax.fori_loop(0, qi - 1, pair, init)
        last = 2 * qi - 1
        scores(last + 1, sb_ref, 0)
        carries = consume(last, sa_ref, carries, None)
        scores(last + 2, sa_ref, 1)
        carries = consume(last + 1, sb_ref, carries, 0)
        finish(consume(last + 2, sa_ref, carries, 1))


def _attn_call(qp, kp, vt):
    bsz, heads, l_pad, _ = qp.shape
    nq = l_pad // BQ
    hps = HEADS_PER_STEP
    return pl.pallas_call(
        _attn_kernel,
        grid=(bsz, heads // hps, nq),
        in_specs=[
            pl.BlockSpec((1, hps, BQ, LANES), lambda b, h, i: (b, h, i, 0)),
            pl.BlockSpec((1, hps, l_pad, LANES), lambda b, h, i: (b, h, 0, 0)),
            pl.BlockSpec((1, hps, HEAD_DIM, l_pad), lambda b, h, i: (b, h, 0, 0)),
        ],
        out_specs=pl.BlockSpec((1, hps, HEAD_DIM, BQ), lambda b, h, i: (b, h, 0, i)),
        out_shape=jax.ShapeDtypeStruct((bsz, heads, HEAD_DIM, l_pad), jnp.bfloat16),
        scratch_shapes=[pltpu.VMEM((hps, BK, BQ), jnp.float32), pltpu.VMEM((hps, BK, BQ), jnp.float32)],
        compiler_params=pltpu.CompilerParams(
            dimension_semantics=("arbitrary", "arbitrary", "arbitrary"), vmem_limit_bytes=VMEM_LIMIT),
        name="attn",
    )(qp, kp, vt)


def _causal_taps(ext_ref, w_ref, n_taps, k_lo, k_hi, shift_ref=None):
    out = None
    for k in range(k_lo, k_hi):
        start = HALO - (n_taps - 1) + k
        if shift_ref is None or start % SUBLANES == 0:
            window = ext_ref[pl.ds(start, ROW_TILE), :]
        else:
            window = shift_ref[start % SUBLANES - 1, pl.ds(start - start % SUBLANES, ROW_TILE), :]
        term = window * w_ref[k:k + 1, :]
        out = term if out is None else out + term
    return out


def _zero_of(x):
    bits = pltpu.bitcast(x, jnp.uint32)
    return lax.shift_right_logical(lax.shift_right_logical(bits, jnp.uint32(16)), jnp.uint32(16)).astype(jnp.float32)


def _conv_parts(mixt_ref, halo, wdw_ref, bdw_ref, lng_ref, lnb_ref, wsc_ref, ext_a, ext_c, shift_a, y_ref, sc_ref):
    tap_parts = CONV_PARTS - 2
    bounds = [round(i * CONF_K / tap_parts) for i in range(tap_parts + 1)]
    state = {}

    def first_part(edge):
        ext_a[0:HALO, :] = halo[:, 0:D_CONF] + edge
        ext_a[HALO:, :] = mixt_ref[0, :, 0:D_CONF]
        ext_c[0:HALO, :] = halo[:, D_CONF + D_SC:] + edge
        ext_c[HALO:, :] = mixt_ref[0, :, D_CONF + D_SC:]
        for r in range(1, SUBLANES):
            shift_a[r - 1] = ext_a[pl.ds(r, SHIFT_ROWS), :]
        state["dw"] = bdw_ref[...]

    def middle_part(part, edge):
        state["dw"] = state["dw"] + edge + _causal_taps(ext_a, wdw_ref, CONF_K, bounds[part - 1], bounds[part],
                                                        shift_a)

    def last_part(edge):
        dw = state["dw"] + edge
        mu = jnp.mean(dw, axis=-1, keepdims=True)
        xc = dw - mu
        y = xc * lax.rsqrt(jnp.mean(xc * xc, axis=-1, keepdims=True) + EPS) * lng_ref[...] + lnb_ref[...]
        y = y * jax.nn.sigmoid(y)
        sc = mixt_ref[0, :, D_CONF:D_CONF + D_SC] * _causal_taps(ext_c, wsc_ref, SC_K, 0, SC_K)
        y_ref[...] = y.astype(y_ref.dtype)
        sc_ref[...] = sc.astype(sc_ref.dtype)
        return _zero_of(jnp.max(y, axis=0, keepdims=True) + jnp.max(sc, axis=0, keepdims=True))

    middle = [functools.partial(middle_part, part) for part in range(1, CONV_PARTS - 1)]
    return [first_part] + middle + [last_part]


def _mix_kernel(src_ref, meta_ref, ot_ref, mix_ref, mixn_ref, wdw_ref, bdw_ref, lng_ref, lnb_ref, wpw_ref,
                bpw_ref, wsc_ref, wout_ref, ng_ref, w1_ref, w2_ref, fg_ref, out_ref,
                ext_a, ext_c, shift_a, y_ref, sc_ref, *, first, final):
    conv_parts = functools.partial(_conv_parts, wdw_ref=wdw_ref, bdw_ref=bdw_ref, lng_ref=lng_ref, lnb_ref=lnb_ref,
                                   wsc_ref=wsc_ref, ext_a=ext_a, ext_c=ext_c, shift_a=shift_a,
                                   y_ref=y_ref, sc_ref=sc_ref)

    @pl.when(pl.program_id(1) == 0)
    def _():
        for part in conv_parts(mix_ref, jnp.zeros((HALO, D_CONF + 2 * D_SC), jnp.float32)):
            part(0.0)

    conf = jnp.dot(y_ref[...], wpw_ref[...], preferred_element_type=jnp.float32) + bpw_ref[...]
    sc = sc_ref[...]

    attn = ot_ref[0].T
    mixed = (jnp.dot(attn, wout_ref[0:D_ATTN, :], preferred_element_type=jnp.float32)
             + jnp.dot(conf.astype(jnp.bfloat16), wout_ref[D_ATTN:D_ATTN + D_CONF, :],
                       preferred_element_type=jnp.float32)
             + jnp.dot(sc, wout_ref[D_ATTN + D_CONF:, :], preferred_element_type=jnp.float32))
    h1 = _tile_rows(src_ref, meta_ref, first) + mixed

    next_parts = conv_parts(mixn_ref, mix_ref[0, ROW_TILE - HALO:, :])
    hn = _rms(h1, ng_ref[...]).astype(jnp.bfloat16)
    acc = h1
    for chunk, c0 in enumerate(range(0, D_FF, FF_CHUNK)):
        if chunk == CONV_PARTS:
            hn = hn + jnp.concatenate([conv_zero] * (D_MODEL // D_CONF), axis=1).astype(jnp.bfloat16)
        a = jnp.dot(hn, w1_ref[:, c0:c0 + FF_CHUNK], preferred_element_type=jnp.float32)
        if chunk < CONV_PARTS:
            conv_zero = next_parts[chunk](_zero_of(a[0:1, 0:D_CONF]))
        a = jnp.maximum(a, 0.0)
        acc = acc + jnp.dot((a * a).astype(jnp.bfloat16), w2_ref[c0:c0 + FF_CHUNK, :],
                            preferred_element_type=jnp.float32)
    if final:
        out_ref[0] = _rms(acc, fg_ref[...])
    else:
        out_ref[0] = jnp.where(_pad_rows_below(acc.shape), 0.0, acc)


def _mix_call(src, meta_tile, ot, mixin, wdw, bdw, lng, lnb, wpw, bpw, wsc, wout, ng, w1, w2, fg,
              *, first, final, l_pad):
    bsz = src.shape[0]
    nt = l_pad // ROW_TILE
    const = lambda b, i: (0, 0)
    once = pl.Buffered(1)
    wide = D_CONF + 2 * D_SC
    if final:
        out_spec = pl.BlockSpec((1, ROW_TILE, D_MODEL), lambda b, i: (b, jnp.maximum(i - 1, 0), 0))
        out_shape = jax.ShapeDtypeStruct((bsz, l_pad - ROW_TILE, D_MODEL), jnp.float32)
    else:
        out_spec = pl.BlockSpec((1, ROW_TILE, D_MODEL), lambda b, i: (b, i, 0))
        out_shape = jax.ShapeDtypeStruct((bsz, l_pad, D_MODEL), jnp.float32)
    return pl.pallas_call(
        functools.partial(_mix_kernel, first=first, final=final),
        grid=(bsz, nt),
        in_specs=[
            _src_spec(first),
            pl.BlockSpec((ROW_TILE, D_MODEL), const),
            pl.BlockSpec((1, D_ATTN, ROW_TILE), lambda b, i: (b, 0, i)),
            pl.BlockSpec((1, ROW_TILE, wide), lambda b, i: (b, i, 0)),
            pl.BlockSpec((1, ROW_TILE, wide), lambda b, i: (b, jnp.minimum(i + 1, nt - 1), 0)),
            pl.BlockSpec((HALO, D_CONF), const),
            pl.BlockSpec((1, D_CONF), const),
            pl.BlockSpec((1, D_CONF), const),
            pl.BlockSpec((1, D_CONF), const),
            pl.BlockSpec((D_CONF, D_CONF), const),
            pl.BlockSpec((1, D_CONF), const),
            pl.BlockSpec((8, D_SC), const),
            pl.BlockSpec((D_MODEL, D_MODEL), const, pipeline_mode=once),
            pl.BlockSpec((1, D_MODEL), const),
            pl.BlockSpec((D_MODEL, D_FF), const, pipeline_mode=once),
            pl.BlockSpec((D_FF, D_MODEL), const, pipeline_mode=once),
            pl.BlockSpec((1, D_MODEL), const),
        ],
        out_specs=out_spec,
        out_shape=out_shape,
        scratch_shapes=[pltpu.VMEM((HALO + ROW_TILE, D_CONF), jnp.float32),
                        pltpu.VMEM((HALO + ROW_TILE, D_SC), jnp.float32),
                        pltpu.VMEM((SUBLANES - 1, SHIFT_ROWS, D_CONF), jnp.float32),
                        pltpu.VMEM((ROW_TILE, D_CONF), jnp.bfloat16),
                        pltpu.VMEM((ROW_TILE, D_SC), jnp.bfloat16)],
        compiler_params=pltpu.CompilerParams(
            dimension_semantics=("arbitrary", "arbitrary"), vmem_limit_bytes=VMEM_LIMIT),
        name="mix_final" if final else "mix",
    )(src, meta_tile, ot, mixin, mixin, wdw, bdw, lng, lnb, wpw, bpw, wsc, wout, ng, w1, w2, fg)


def _decay_selectors():
    assert HEAD_DIM + HEADS * DECAY_LANES <= LANES
    e = np.zeros((N_SPLIT * LANES, 2 * LANES), np.float32)
    bias = np.zeros((1, 2 * LANES), np.float32)
    lane_rows = np.zeros((2 * HEADS, LANES), np.float32)
    for hd in range(HEADS):
        lane0 = HEAD_DIM + hd * DECAY_LANES
        lane_rows[hd, lane0:lane0 + DECAY_LANES] = 1.0
        lane_rows[HEADS, lane0 + N_SPLIT] = PAD_KEY_BIAS
        for j in range(N_SPLIT):
            e[j * LANES + hd, lane0 + j] = 1.0
            bias[0, lane0 + N_SPLIT + j] = 1.0
            bias[0, LANES + lane0 + j] = 1.0
            e[j * LANES + hd, LANES + lane0 + N_SPLIT + j] = -1.0
    return jnp.asarray(e, jnp.bfloat16), jnp.asarray(bias, jnp.float32), jnp.asarray(lane_rows, jnp.float32)


def _pack_w_in(w):
    q = w[:, 0:D_ATTN] * (HEAD_DIM ** -0.5)
    k = w[:, D_ATTN:2 * D_ATTN]
    v = w[:, 2 * D_ATTN:3 * D_ATTN]
    f = jnp.pad(w[:, 3 * D_ATTN:3 * D_ATTN + HEADS], ((0, 0), (0, LANES - HEADS)))
    rest = w[:, 3 * D_ATTN + HEADS:]
    return jnp.concatenate([f, q, k, v, rest], -1).astype(jnp.bfloat16)


def kernel(x, meta_tokens, mix_norm_g, w_in, b_forget, w_conf_dw, b_conf_dw, conf_ln_g, conf_ln_b,
           w_conf_pw, b_conf_pw, w_sc_conv, w_out, mlp_norm_g, w_mlp1, w_mlp2, final_norm_g):
    bsz, seq, d = x.shape
    depth = w_in.shape[0]
    assert seq % ROW_TILE == 0 and meta_tokens.shape[0] == N_META and PAD_FRONT >= BK and BQ == 2 * BK
    l_pad = ROW_TILE + seq

    meta_tile = jnp.concatenate([jnp.zeros((PAD_FRONT, d), x.dtype), meta_tokens.astype(x.dtype)], axis=0)
    tri = jnp.asarray(np.tril(np.ones((ROW_TILE, ROW_TILE), np.float32)), jnp.bfloat16)
    e_mat, qk_bias, lane_rows = _decay_selectors()
    row = lambda a: a.reshape(1, -1).astype(jnp.float32)
    fg = row(final_norm_g)

    h = x
    for l in range(depth):
        first, final = l == 0, l == depth - 1
        bf_row = jnp.pad(row(b_forget[l]), ((0, 0), (0, LANES - HEADS)))
        qp, kp, vt, mixin = _proj_call(h, meta_tile, row(mix_norm_g[l]), _pack_w_in(w_in[l]), bf_row, tri, e_mat,
                                       qk_bias, lane_rows, first=first, l_pad=l_pad)
        ot = _attn_call(qp, kp, vt).reshape(bsz, D_ATTN, l_pad)
        wdw = jnp.pad(w_conf_dw[l], ((0, HALO - CONF_K), (0, 0)))
        wsc = jnp.pad(w_sc_conv[l], ((0, 8 - SC_K), (0, 0)))
        h = _mix_call(h, meta_tile, ot, mixin, wdw, row(b_conf_dw[l]), row(conf_ln_g[l]), row(conf_ln_b[l]),
                      w_conf_pw[l].astype(jnp.bfloat16), row(b_conf_pw[l]), wsc,
                      w_out[l].astype(jnp.bfloat16), row(mlp_norm_g[l]),
                      w_mlp1[l].astype(jnp.bfloat16), w_mlp2[l].astype(jnp.bfloat16), fg,
                      first=first, final=final, l_pad=l_pad)
    return h
```

```python
import functools

import jax
import jax.numpy as jnp
import numpy as np
from jax import lax
from jax.experimental import pallas as pl
from jax.experimental.pallas import tpu as pltpu

D_MODEL = 1024
N_META = 16
HEADS = 8
HEAD_DIM = 64
D_ATTN = HEADS * HEAD_DIM
D_CONF = 256
D_SC = 256
CONF_K = 31
SC_K = 3
D_FF = 4 * D_MODEL
EPS = 1e-6

LANES = 128
SUBLANES = 8
ROW_TILE = 512
PAD_FRONT = ROW_TILE - N_META
BQ = ROW_TILE
BK = 256
HEADS_PER_STEP = 2
LOG2E = 1.4426950408889634
PAD_KEY_BIAS = -2.0 ** 100
HALO = 32
SHIFT_ROWS = ROW_TILE + HALO - SUBLANES
FF_CHUNK = 512
CONV_PARTS = 7
N_SPLIT = 3
VMEM_LIMIT = 56 * 1024 * 1024

C_F = 0
C_Q = C_F + LANES
C_K = C_Q + D_ATTN
C_V = C_K + D_ATTN
C_REST = C_V + D_ATTN
N_PACK = C_REST + 2 * D_CONF + 3 * D_SC
DECAY_LANES = 2 * N_SPLIT


def _split_bf16(x):
    pieces = []
    r = x
    for _ in range(N_SPLIT):
        p = r.astype(jnp.bfloat16)
        pieces.append(p)
        r = r - p.astype(jnp.float32)
    return pieces


def _rms(x, g):
    return x * lax.rsqrt(jnp.mean(x * x, axis=-1, keepdims=True) + EPS) * g


def _tile_rows(src_ref, meta_ref, first):
    h = src_ref[0]
    if first:
        h = jnp.where(pl.program_id(1) == 0, meta_ref[...], h)
    return h


def _pad_rows_below(shape):
    limit = jnp.where(pl.program_id(1) == 0, PAD_FRONT, 0)
    return lax.broadcasted_iota(jnp.int32, shape, 0) < limit


def _proj_kernel(src_ref, meta_ref, g_ref, w_ref, bf_ref, tri_ref, e_ref, qkb_ref, lanes_ref,
                 qp_ref, kp_ref, vt_ref, mix_ref, carry_ref, *, first):
    @pl.when(pl.program_id(1) == 0)
    def _():
        carry_ref[...] = jnp.zeros_like(carry_ref)

    hn = _rms(_tile_rows(src_ref, meta_ref, first), g_ref[...]).astype(jnp.bfloat16)

    fq = jnp.dot(hn, w_ref[:, C_F:C_K], preferred_element_type=jnp.float32)
    z = fq[:, :LANES] + bf_ref[...]
    log_f = jnp.minimum(z, 0.0) - jnp.log1p(jnp.exp(-jnp.abs(z)))
    sums = jnp.dot(tri_ref[...], jnp.concatenate(_split_bf16(log_f), axis=-1),
                   preferred_element_type=jnp.float32)
    c = carry_ref[0:1, :]
    for piece in range(N_SPLIT):
        c = c + sums[:, piece * LANES:(piece + 1) * LANES]
    carry_ref[0:1, :] = c[ROW_TILE - 1:ROW_TILE, :]

    c_pieces = jnp.concatenate(_split_bf16(c * LOG2E), axis=-1)
    decay = jnp.dot(c_pieces, e_ref[...], preferred_element_type=jnp.float32) + qkb_ref[...]
    decay_q, decay_k = decay[:, :LANES], decay[:, LANES:]
    decay_k = decay_k + jnp.where(_pad_rows_below((ROW_TILE, LANES)), lanes_ref[HEADS:HEADS + 1, :], 0.0)
    q = fq[:, LANES:] * LOG2E
    k = jnp.dot(hn, w_ref[:, C_K:C_V], preferred_element_type=jnp.float32)
    low_half = lax.broadcasted_iota(jnp.int32, (ROW_TILE, LANES), 1) < HEAD_DIM
    for hd in range(HEADS):
        pair_cols = slice((hd // 2) * LANES, (hd // 2 + 1) * LANES)
        q_h, k_h = q[:, pair_cols], k[:, pair_cols]
        if hd % 2:
            q_h, k_h = pltpu.roll(q_h, HEAD_DIM, axis=1), pltpu.roll(k_h, HEAD_DIM, axis=1)
        qp_ref[0, hd] = jnp.where(low_half, q_h, decay_q).astype(jnp.bfloat16)
        kp_ref[0, hd] = jnp.where(low_half, k_h, decay_k * lanes_ref[hd:hd + 1, :]).astype(jnp.bfloat16)

    v = jnp.dot(hn, w_ref[:, C_V:C_REST], preferred_element_type=jnp.float32)
    vt = v.T.astype(jnp.bfloat16)
    for hd in range(HEADS):
        vt_ref[0, hd] = vt[hd * HEAD_DIM:(hd + 1) * HEAD_DIM, :]

    r = jnp.dot(hn, w_ref[:, C_REST:N_PACK], preferred_element_type=jnp.float32)
    conf_a = r[:, 0:D_CONF]
    conf_g = r[:, D_CONF:2 * D_CONF]
    sc_b = r[:, 2 * D_CONF:2 * D_CONF + D_SC]
    sc_c = r[:, 2 * D_CONF + D_SC:2 * D_CONF + 2 * D_SC]
    sc_u = r[:, 2 * D_CONF + 2 * D_SC:]
    mix_ref[0, :, 0:D_CONF] = conf_a * jax.nn.sigmoid(conf_g)
    mix_ref[0, :, D_CONF:D_CONF + D_SC] = sc_b
    mix_ref[0, :, D_CONF + D_SC:] = sc_c * sc_u


def _src_spec(first):
    if first:
        return pl.BlockSpec((1, ROW_TILE, D_MODEL), lambda b, i: (b, jnp.maximum(i - 1, 0), 0))
    return pl.BlockSpec((1, ROW_TILE, D_MODEL), lambda b, i: (b, i, 0))


def _layer_spec(shape, layer, **kwargs):
    return pl.BlockSpec((None,) + shape, lambda b, i: (layer,) + (0,) * len(shape), **kwargs)


def _proj_call(src, meta_tile, g, w_pack, bf_row, tri, e_mat, qk_bias, lane_rows, *, layer, first, l_pad):
    bsz = src.shape[0]
    nt = l_pad // ROW_TILE
    const = lambda b, i: (0, 0)
    return pl.pallas_call(
        functools.partial(_proj_kernel, first=first),
        grid=(bsz, nt),
        in_specs=[
            _src_spec(first),
            pl.BlockSpec((ROW_TILE, D_MODEL), const),
            _layer_spec((1, D_MODEL), layer),
            _layer_spec((D_MODEL, N_PACK), layer),
            _layer_spec((1, LANES), layer),
            pl.BlockSpec((ROW_TILE, ROW_TILE), const),
            pl.BlockSpec((N_SPLIT * LANES, 2 * LANES), const),
            pl.BlockSpec((1, 2 * LANES), const),
            pl.BlockSpec((2 * HEADS, LANES), const),
        ],
        out_specs=[
            pl.BlockSpec((1, HEADS, ROW_TILE, LANES), lambda b, i: (b, 0, i, 0)),
            pl.BlockSpec((1, HEADS, ROW_TILE, LANES), lambda b, i: (b, 0, i, 0)),
            pl.BlockSpec((1, HEADS, HEAD_DIM, ROW_TILE), lambda b, i: (b, 0, 0, i)),
            pl.BlockSpec((1, ROW_TILE, D_CONF + 2 * D_SC), lambda b, i: (b, i, 0)),
        ],
        out_shape=[
            jax.ShapeDtypeStruct((bsz, HEADS, l_pad, LANES), jnp.bfloat16),
            jax.ShapeDtypeStruct((bsz, HEADS, l_pad, LANES), jnp.bfloat16),
            jax.ShapeDtypeStruct((bsz, HEADS, HEAD_DIM, l_pad), jnp.bfloat16),
            jax.ShapeDtypeStruct((bsz, l_pad, D_CONF + 2 * D_SC), jnp.float32),
        ],
        scratch_shapes=[pltpu.VMEM((8, LANES), jnp.float32)],
        compiler_params=pltpu.CompilerParams(
            dimension_semantics=("arbitrary", "arbitrary"), vmem_limit_bytes=VMEM_LIMIT),
        name="proj_first" if first else "proj",
    )(src, meta_tile, g, w_pack, bf_row, tri, e_mat, qk_bias, lane_rows)


def _attn_kernel(qp_ref, kp_ref, vt_ref, o_ref, sa_ref, sb_ref):
    qi = pl.program_id(2)
    heads = range(HEADS_PER_STEP)
    qs = [qp_ref[0, hd] for hd in heads]

    def scores(j, s_ref, diag=None):
        k0 = pl.multiple_of(j * BK, BK)
        q0 = 0 if diag is None else diag * BK
        for hd in heads:
            s_ref[hd, :, q0:] = lax.dot_general(kp_ref[0, hd, pl.ds(k0, BK), :], qs[hd][q0:],
                                                (((1,), (1,)), ((), ())),
                                                preferred_element_type=jnp.float32)

    def consume(j, s_ref, carries, diag):
        k0 = pl.multiple_of(j * BK, BK)
        q0 = 0 if diag is None else diag * BK
        out = []
        for hd in heads:
            m, l, acc = (c[:, q0:] for c in carries[hd])
            s = s_ref[hd, :, q0:]
            if diag is not None:
                kpos = lax.broadcasted_iota(jnp.int32, s.shape, 0) + diag * BK
                qpos = lax.broadcasted_iota(jnp.int32, s.shape, 1) + q0
                s = jnp.where(kpos <= qpos, s, -jnp.inf)
            m_new = jnp.maximum(m, jnp.max(s, axis=0, keepdims=True))
            alpha = jnp.exp2(m - m_new)
            p = jnp.exp2(s - m_new)
            l = alpha * l + jnp.sum(p, axis=0, keepdims=True)
            vblk = vt_ref[0, hd, :, pl.ds(k0, BK)]
            acc = alpha * acc + jnp.dot(vblk, p.astype(jnp.bfloat16),
                                        preferred_element_type=jnp.float32)
            new = (m_new, l, acc)
            if q0:
                new = tuple(jnp.concatenate([c[:, :q0], n], axis=1) for c, n in zip(carries[hd], new))
            out.append(new)
        return tuple(out)

    def finish(carries):
        for hd, (m, l, acc) in enumerate(carries):
            o_ref[0, hd] = (acc / l).astype(o_ref.dtype)

    init = ((jnp.full((1, BQ), -jnp.inf, jnp.float32), jnp.zeros((1, BQ), jnp.float32),
             jnp.zeros((HEAD_DIM, BQ), jnp.float32)),) * HEADS_PER_STEP

    @pl.when(qi == 0)
    def _():
        scores(0, sa_ref, 0)
        scores(1, sb_ref, 1)
        finish(consume(1, sb_ref, consume(0, sa_ref, init, 0), 1))

    @pl.when(qi > 0)
    def _():
        def pair(t, carries):
            j = 2 * t + 1
            scores(j + 1, sb_ref)
            carries = consume(j, sa_ref, carries, None)
            scores(j + 2, sa_ref)
            return consume(j + 1, sb_ref, carries, None)

        scores(1, sa_ref)
        carries = lax.fori_loop(0, qi - 1, pair, init)
        last = 2 * qi - 1
        scores(last + 1, sb_ref, 0)
        carries = consume(last, sa_ref, carries, None)
        scores(last + 2, sa_ref, 1)
        carries = consume(last + 1, sb_ref, carries, 0)
        finish(consume(last + 2, sa_ref, carries, 1))


def _attn_call(qp, kp, vt):
    bsz, heads, l_pad, _ = qp.shape
    nq = l_pad // BQ
    hps = HEADS_PER_STEP
    return pl.pallas_call(
        _attn_kernel,
        grid=(bsz, heads // hps, nq),
        in_specs=[
            pl.BlockSpec((1, hps, BQ, LANES), lambda b, h, i: (b, h, i, 0)),
            pl.BlockSpec((1, hps, l_pad, LANES), lambda b, h, i: (b, h, 0, 0)),
            pl.BlockSpec((1, hps, HEAD_DIM, l_pad), lambda b, h, i: (b, h, 0, 0)),
        ],
        out_specs=pl.BlockSpec((1, hps, HEAD_DIM, BQ), lambda b, h, i: (b, h, 0, i)),
        out_shape=jax.ShapeDtypeStruct((bsz, heads, HEAD_DIM, l_pad), jnp.bfloat16),
        scratch_shapes=[pltpu.VMEM((hps, BK, BQ), jnp.float32), pltpu.VMEM((hps, BK, BQ), jnp.float32)],
        compiler_params=pltpu.CompilerParams(
            dimension_semantics=("arbitrary", "arbitrary", "arbitrary"), vmem_limit_bytes=VMEM_LIMIT),
        name="attn",
    )(qp, kp, vt)


def _causal_taps(ext_ref, w_ref, n_taps, k_lo, k_hi, shift_ref=None):
    out = None
    for k in range(k_lo, k_hi):
        start = HALO - (n_taps - 1) + k
        if shift_ref is None or start % SUBLANES == 0:
            window = ext_ref[pl.ds(start, ROW_TILE), :]
        else:
            window = shift_ref[start % SUBLANES - 1, pl.ds(start - start % SUBLANES, ROW_TILE), :]
        term = window * w_ref[k:k + 1, :]
        out = term if out is None else out + term
    return out


def _zero_of(x):
    bits = pltpu.bitcast(x, jnp.uint32)
    return lax.shift_right_logical(lax.shift_right_logical(bits, jnp.uint32(16)), jnp.uint32(16)).astype(jnp.float32)


def _conv_parts(mixt_ref, halo, wdw_ref, bdw_ref, lng_ref, lnb_ref, wsc_ref, ext_a, ext_c, shift_a, y_ref, sc_ref):
    tap_parts = CONV_PARTS - 2
    bounds = [round(i * CONF_K / tap_parts) for i in range(tap_parts + 1)]
    state = {}

    def first_part(edge):
        ext_a[0:HALO, :] = halo[:, 0:D_CONF] + edge
        ext_a[HALO:, :] = mixt_ref[0, :, 0:D_CONF]
        ext_c[0:HALO, :] = halo[:, D_CONF + D_SC:] + edge
        ext_c[HALO:, :] = mixt_ref[0, :, D_CONF + D_SC:]
        for r in range(1, SUBLANES):
            shift_a[r - 1] = ext_a[pl.ds(r, SHIFT_ROWS), :]
        state["dw"] = bdw_ref[...]

    def middle_part(part, edge):
        state["dw"] = state["dw"] + edge + _causal_taps(ext_a, wdw_ref, CONF_K, bounds[part - 1], bounds[part],
                                                        shift_a)

    def last_part(edge):
        dw = state["dw"] + edge
        mu = jnp.mean(dw, axis=-1, keepdims=True)
        xc = dw - mu
        y = xc * lax.rsqrt(jnp.mean(xc * xc, axis=-1, keepdims=True) + EPS) * lng_ref[...] + lnb_ref[...]
        y = y * jax.nn.sigmoid(y)
        sc = mixt_ref[0, :, D_CONF:D_CONF + D_SC] * _causal_taps(ext_c, wsc_ref, SC_K, 0, SC_K)
        y_ref[...] = y.astype(y_ref.dtype)
        sc_ref[...] = sc.astype(sc_ref.dtype)
        return _zero_of(jnp.max(y, axis=0, keepdims=True) + jnp.max(sc, axis=0, keepdims=True))

    middle = [functools.partial(middle_part, part) for part in range(1, CONV_PARTS - 1)]
    return [first_part] + middle + [last_part]


def _mix_kernel(src_ref, meta_ref, ot_ref, mix_ref, mixn_ref, wdw_ref, bdw_ref, lng_ref, lnb_ref, wpw_ref,
                bpw_ref, wsc_ref, wout_ref, ng_ref, w1_ref, w2_ref, fg_ref, out_ref,
                ext_a, ext_c, shift_a, y_ref, sc_ref, *, first, final):
    conv_parts = functools.partial(_conv_parts, wdw_ref=wdw_ref, bdw_ref=bdw_ref, lng_ref=lng_ref, lnb_ref=lnb_ref,
                                   wsc_ref=wsc_ref, ext_a=ext_a, ext_c=ext_c, shift_a=shift_a,
                                   y_ref=y_ref, sc_ref=sc_ref)

    @pl.when(pl.program_id(1) == 0)
    def _():
        for part in conv_parts(mix_ref, jnp.zeros((HALO, D_CONF + 2 * D_SC), jnp.float32)):
            part(0.0)

    conf = jnp.dot(y_ref[...], wpw_ref[...], preferred_element_type=jnp.float32) + bpw_ref[...]
    sc = sc_ref[...]

    attn = ot_ref[0].T
    mixed = (jnp.dot(attn, wout_ref[0:D_ATTN, :], preferred_element_type=jnp.float32)
             + jnp.dot(conf.astype(jnp.bfloat16), wout_ref[D_ATTN:D_ATTN + D_CONF, :],
                       preferred_element_type=jnp.float32)
             + jnp.dot(sc, wout_ref[D_ATTN + D_CONF:, :], preferred_element_type=jnp.float32))
    h1 = _tile_rows(src_ref, meta_ref, first) + mixed

    next_parts = conv_parts(mixn_ref, mix_ref[0, ROW_TILE - HALO:, :])
    hn = _rms(h1, ng_ref[...]).astype(jnp.bfloat16)
    acc = h1
    for chunk, c0 in enumerate(range(0, D_FF, FF_CHUNK)):
        if chunk == CONV_PARTS:
            hn = hn + jnp.concatenate([conv_zero] * (D_MODEL // D_CONF), axis=1).astype(jnp.bfloat16)
        a = jnp.dot(hn, w1_ref[:, c0:c0 + FF_CHUNK], preferred_element_type=jnp.float32)
        if chunk < CONV_PARTS:
            conv_zero = next_parts[chunk](_zero_of(a[0:1, 0:D_CONF]))
        a = jnp.maximum(a, 0.0)
        acc = acc + jnp.dot((a * a).astype(jnp.bfloat16), w2_ref[c0:c0 + FF_CHUNK, :],
                            preferred_element_type=jnp.float32)
    if final:
        out_ref[0] = _rms(acc, fg_ref[...])
    else:
        out_ref[0] = jnp.where(_pad_rows_below(acc.shape), 0.0, acc)


def _mix_call(src, meta_tile, ot, mixin, wdw, bdw, lng, lnb, wpw, bpw, wsc, wout, ng, w1, w2, fg,
              *, layer, first, final, l_pad):
    bsz = src.shape[0]
    nt = l_pad // ROW_TILE
    const = lambda b, i: (0, 0)
    once = pl.Buffered(1)
    wide = D_CONF + 2 * D_SC
    if final:
        out_spec = pl.BlockSpec((1, ROW_TILE, D_MODEL), lambda b, i: (b, jnp.maximum(i - 1, 0), 0))
        out_shape = jax.ShapeDtypeStruct((bsz, l_pad - ROW_TILE, D_MODEL), jnp.float32)
    else:
        out_spec = pl.BlockSpec((1, ROW_TILE, D_MODEL), lambda b, i: (b, i, 0))
        out_shape = jax.ShapeDtypeStruct((bsz, l_pad, D_MODEL), jnp.float32)
    return pl.pallas_call(
        functools.partial(_mix_kernel, first=first, final=final),
        grid=(bsz, nt),
        in_specs=[
            _src_spec(first),
            pl.BlockSpec((ROW_TILE, D_MODEL), const),
            pl.BlockSpec((1, D_ATTN, ROW_TILE), lambda b, i: (b, 0, i)),
            pl.BlockSpec((1, ROW_TILE, wide), lambda b, i: (b, i, 0)),
            pl.BlockSpec((1, ROW_TILE, wide), lambda b, i: (b, jnp.minimum(i + 1, nt - 1), 0)),
            _layer_spec((HALO, D_CONF), layer),
            _layer_spec((1, D_CONF), layer),
            _layer_spec((1, D_CONF), layer),
            _layer_spec((1, D_CONF), layer),
            _layer_spec((D_CONF, D_CONF), layer),
            _layer_spec((1, D_CONF), layer),
            _layer_spec((SUBLANES, D_SC), layer),
            _layer_spec((D_MODEL, D_MODEL), layer, pipeline_mode=once),
            _layer_spec((1, D_MODEL), layer),
            _layer_spec((D_MODEL, D_FF), layer, pipeline_mode=once),
            _layer_spec((D_FF, D_MODEL), layer, pipeline_mode=once),
            pl.BlockSpec((1, D_MODEL), const),
        ],
        out_specs=out_spec,
        out_shape=out_shape,
        scratch_shapes=[pltpu.VMEM((HALO + ROW_TILE, D_CONF), jnp.float32),
                        pltpu.VMEM((HALO + ROW_TILE, D_SC), jnp.float32),
                        pltpu.VMEM((SUBLANES - 1, SHIFT_ROWS, D_CONF), jnp.float32),
                        pltpu.VMEM((ROW_TILE, D_CONF), jnp.bfloat16),
                        pltpu.VMEM((ROW_TILE, D_SC), jnp.bfloat16)],
        compiler_params=pltpu.CompilerParams(
            dimension_semantics=("arbitrary", "arbitrary"), vmem_limit_bytes=VMEM_LIMIT),
        name="mix_final" if final else "mix",
    )(src, meta_tile, ot, mixin, mixin, wdw, bdw, lng, lnb, wpw, bpw, wsc, wout, ng, w1, w2, fg)


def _decay_selectors():
    assert HEAD_DIM + HEADS * DECAY_LANES <= LANES
    e = np.zeros((N_SPLIT * LANES, 2 * LANES), np.float32)
    bias = np.zeros((1, 2 * LANES), np.float32)
    lane_rows = np.zeros((2 * HEADS, LANES), np.float32)
    for hd in range(HEADS):
        lane0 = HEAD_DIM + hd * DECAY_LANES
        lane_rows[hd, lane0:lane0 + DECAY_LANES] = 1.0
        lane_rows[HEADS, lane0 + N_SPLIT] = PAD_KEY_BIAS
        for j in range(N_SPLIT):
            e[j * LANES + hd, lane0 + j] = 1.0
            bias[0, lane0 + N_SPLIT + j] = 1.0
            bias[0, LANES + lane0 + j] = 1.0
            e[j * LANES + hd, LANES + lane0 + N_SPLIT + j] = -1.0
    return jnp.asarray(e, jnp.bfloat16), jnp.asarray(bias, jnp.float32), jnp.asarray(lane_rows, jnp.float32)


def _pack_w_in(w):
    q = w[..., 0:D_ATTN] * (HEAD_DIM ** -0.5)
    k = w[..., D_ATTN:2 * D_ATTN]
    v = w[..., 2 * D_ATTN:3 * D_ATTN]
    f = jnp.pad(w[..., 3 * D_ATTN:3 * D_ATTN + HEADS], ((0, 0), (0, 0), (0, LANES - HEADS)))
    rest = w[..., 3 * D_ATTN + HEADS:]
    return jnp.concatenate([f, q, k, v, rest], -1).astype(jnp.bfloat16)


def kernel(x, meta_tokens, mix_norm_g, w_in, b_forget, w_conf_dw, b_conf_dw, conf_ln_g, conf_ln_b,
           w_conf_pw, b_conf_pw, w_sc_conv, w_out, mlp_norm_g, w_mlp1, w_mlp2, final_norm_g):
    bsz, seq, d = x.shape
    depth = w_in.shape[0]
    assert seq % ROW_TILE == 0 and meta_tokens.shape[0] == N_META and PAD_FRONT >= BK and BQ == 2 * BK
    l_pad = ROW_TILE + seq

    meta_tile = jnp.concatenate([jnp.zeros((PAD_FRONT, d), x.dtype), meta_tokens.astype(x.dtype)], axis=0)
    tri = jnp.asarray(np.tril(np.ones((ROW_TILE, ROW_TILE), np.float32)), jnp.bfloat16)
    e_mat, qk_bias, lane_rows = _decay_selectors()

    rows = lambda a: a[:, None, :].astype(jnp.float32)
    bf16 = lambda a: a.astype(jnp.bfloat16)
    w_pack, bf_rows = _pack_w_in(w_in), jnp.pad(rows(b_forget), ((0, 0), (0, 0), (0, LANES - HEADS)))
    wdw = jnp.pad(w_conf_dw, ((0, 0), (0, HALO - CONF_K), (0, 0)))
    wsc = jnp.pad(w_sc_conv, ((0, 0), (0, SUBLANES - SC_K), (0, 0)))
    mix_params = (wdw, rows(b_conf_dw), rows(conf_ln_g), rows(conf_ln_b), bf16(w_conf_pw), rows(b_conf_pw), wsc,
                  bf16(w_out), rows(mlp_norm_g), bf16(w_mlp1), bf16(w_mlp2),
                  final_norm_g.reshape(1, -1).astype(jnp.float32))
    g_rows = rows(mix_norm_g)

    h = x
    for l in range(depth):
        first, final = l == 0, l == depth - 1
        qp, kp, vt, mixin = _proj_call(h, meta_tile, g_rows, w_pack, bf_rows, tri, e_mat, qk_bias, lane_rows,
                                       layer=l, first=first, l_pad=l_pad)
        ot = _attn_call(qp, kp, vt).reshape(bsz, D_ATTN, l_pad)
        h = _mix_call(h, meta_tile, ot, mixin, *mix_params, layer=l, first=first, final=final, l_pad=l_pad)
    return h
```

```python
import functools

import jax
import jax.numpy as jnp
import numpy as np
from jax import lax
from jax.experimental import pallas as pl
from jax.experimental.pallas import tpu as pltpu

D_MODEL = 1024
N_META = 16
HEADS = 8
HEAD_DIM = 64
D_ATTN = HEADS * HEAD_DIM
D_CONF = 256
D_SC = 256
CONF_K = 31
SC_K = 3
D_FF = 4 * D_MODEL
EPS = 1e-6

LANES = 128
SUBLANES = 8
ROW_TILE = 512
PAD_FRONT = ROW_TILE - N_META
BQ = ROW_TILE
BK = 256
HEADS_PER_STEP = 2
LOG2E = 1.4426950408889634
VT_ROWS = HEAD_DIM + 16
PAD_KEY_BIAS = -2.0 ** 100
HALO = 32
SHIFT_ROWS = ROW_TILE + HALO - SUBLANES
FF_CHUNK = 512
CONV_PARTS = 7
N_SPLIT = 3
VMEM_LIMIT = 56 * 1024 * 1024

C_F = 0
C_Q = C_F + LANES
C_K = C_Q + D_ATTN
C_V = C_K + D_ATTN
C_REST = C_V + D_ATTN
N_PACK = C_REST + 2 * D_CONF + 3 * D_SC
DECAY_LANES = 2 * N_SPLIT


def _split_bf16(x):
    pieces = []
    r = x
    for _ in range(N_SPLIT):
        p = r.astype(jnp.bfloat16)
        pieces.append(p)
        r = r - p.astype(jnp.float32)
    return pieces


def _rms(x, g):
    return x * lax.rsqrt(jnp.mean(x * x, axis=-1, keepdims=True) + EPS) * g


def _tile_rows(src_ref, meta_ref, first):
    h = src_ref[0]
    if first:
        h = jnp.where(pl.program_id(1) == 0, meta_ref[...], h)
    return h


def _pad_rows_below(shape):
    limit = jnp.where(pl.program_id(1) == 0, PAD_FRONT, 0)
    return lax.broadcasted_iota(jnp.int32, shape, 0) < limit


def _proj_kernel(src_ref, meta_ref, g_ref, w_ref, bf_ref, tri_ref, e_ref, qkb_ref, lanes_ref,
                 qp_ref, kp_ref, vt_ref, mix_ref, carry_ref, *, first):
    @pl.when(pl.program_id(1) == 0)
    def _():
        carry_ref[...] = jnp.zeros_like(carry_ref)

    hn = _rms(_tile_rows(src_ref, meta_ref, first), g_ref[...]).astype(jnp.bfloat16)

    fq = jnp.dot(hn, w_ref[:, C_F:C_K], preferred_element_type=jnp.float32)
    z = fq[:, :LANES] + bf_ref[...]
    log_f = jnp.minimum(z, 0.0) - jnp.log1p(jnp.exp(-jnp.abs(z)))
    sums = jnp.dot(tri_ref[...], jnp.concatenate(_split_bf16(log_f), axis=-1),
                   preferred_element_type=jnp.float32)
    c = carry_ref[0:1, :]
    for piece in range(N_SPLIT):
        c = c + sums[:, piece * LANES:(piece + 1) * LANES]
    carry_ref[0:1, :] = c[ROW_TILE - 1:ROW_TILE, :]

    c_pieces = jnp.concatenate(_split_bf16(c * LOG2E), axis=-1)
    decay = jnp.dot(c_pieces, e_ref[...], preferred_element_type=jnp.float32) + qkb_ref[...]
    decay_q, decay_k = decay[:, :LANES], decay[:, LANES:]
    decay_k = decay_k + jnp.where(_pad_rows_below((ROW_TILE, LANES)), lanes_ref[HEADS:HEADS + 1, :], 0.0)
    q = fq[:, LANES:] * LOG2E
    k = jnp.dot(hn, w_ref[:, C_K:C_V], preferred_element_type=jnp.float32)
    low_half = lax.broadcasted_iota(jnp.int32, (ROW_TILE, LANES), 1) < HEAD_DIM
    for hd in range(HEADS):
        pair_cols = slice((hd // 2) * LANES, (hd // 2 + 1) * LANES)
        q_h, k_h = q[:, pair_cols], k[:, pair_cols]
        if hd % 2:
            q_h, k_h = pltpu.roll(q_h, HEAD_DIM, axis=1), pltpu.roll(k_h, HEAD_DIM, axis=1)
        qp_ref[0, hd] = jnp.where(low_half, q_h, decay_q).astype(jnp.bfloat16)
        kp_ref[0, hd] = jnp.where(low_half, k_h, decay_k * lanes_ref[hd:hd + 1, :]).astype(jnp.bfloat16)

    v = jnp.dot(hn, w_ref[:, C_V:C_REST], preferred_element_type=jnp.float32)
    vt = v.T.astype(jnp.bfloat16)
    ones_row = (lax.broadcasted_iota(jnp.int32, (VT_ROWS - HEAD_DIM, ROW_TILE), 0) == 0).astype(jnp.bfloat16)
    for hd in range(HEADS):
        vt_ref[0, hd, 0:HEAD_DIM, :] = vt[hd * HEAD_DIM:(hd + 1) * HEAD_DIM, :]
        vt_ref[0, hd, HEAD_DIM:, :] = ones_row

    r = jnp.dot(hn, w_ref[:, C_REST:N_PACK], preferred_element_type=jnp.float32)
    conf_a = r[:, 0:D_CONF]
    conf_g = r[:, D_CONF:2 * D_CONF]
    sc_b = r[:, 2 * D_CONF:2 * D_CONF + D_SC]
    sc_c = r[:, 2 * D_CONF + D_SC:2 * D_CONF + 2 * D_SC]
    sc_u = r[:, 2 * D_CONF + 2 * D_SC:]
    mix_ref[0, :, 0:D_CONF] = conf_a * jax.nn.sigmoid(conf_g)
    mix_ref[0, :, D_CONF:D_CONF + D_SC] = sc_b
    mix_ref[0, :, D_CONF + D_SC:] = sc_c * sc_u


def _src_spec(first):
    if first:
        return pl.BlockSpec((1, ROW_TILE, D_MODEL), lambda b, i: (b, jnp.maximum(i - 1, 0), 0))
    return pl.BlockSpec((1, ROW_TILE, D_MODEL), lambda b, i: (b, i, 0))


def _layer_spec(shape, layer, **kwargs):
    return pl.BlockSpec((None,) + shape, lambda b, i: (layer,) + (0,) * len(shape), **kwargs)


def _proj_call(src, meta_tile, g, w_pack, bf_row, tri, e_mat, qk_bias, lane_rows, *, layer, first, l_pad):
    bsz = src.shape[0]
    nt = l_pad // ROW_TILE
    const = lambda b, i: (0, 0)
    return pl.pallas_call(
        functools.partial(_proj_kernel, first=first),
        grid=(bsz, nt),
        in_specs=[
            _src_spec(first),
            pl.BlockSpec((ROW_TILE, D_MODEL), const),
            _layer_spec((1, D_MODEL), layer),
            _layer_spec((D_MODEL, N_PACK), layer),
            _layer_spec((1, LANES), layer),
            pl.BlockSpec((ROW_TILE, ROW_TILE), const),
            pl.BlockSpec((N_SPLIT * LANES, 2 * LANES), const),
            pl.BlockSpec((1, 2 * LANES), const),
            pl.BlockSpec((2 * HEADS, LANES), const),
        ],
        out_specs=[
            pl.BlockSpec((1, HEADS, ROW_TILE, LANES), lambda b, i: (b, 0, i, 0)),
            pl.BlockSpec((1, HEADS, ROW_TILE, LANES), lambda b, i: (b, 0, i, 0)),
            pl.BlockSpec((1, HEADS, VT_ROWS, ROW_TILE), lambda b, i: (b, 0, 0, i)),
            pl.BlockSpec((1, ROW_TILE, D_CONF + 2 * D_SC), lambda b, i: (b, i, 0)),
        ],
        out_shape=[
            jax.ShapeDtypeStruct((bsz, HEADS, l_pad, LANES), jnp.bfloat16),
            jax.ShapeDtypeStruct((bsz, HEADS, l_pad, LANES), jnp.bfloat16),
            jax.ShapeDtypeStruct((bsz, HEADS, VT_ROWS, l_pad), jnp.bfloat16),
            jax.ShapeDtypeStruct((bsz, l_pad, D_CONF + 2 * D_SC), jnp.float32),
        ],
        scratch_shapes=[pltpu.VMEM((8, LANES), jnp.float32)],
        compiler_params=pltpu.CompilerParams(
            dimension_semantics=("arbitrary", "arbitrary"), vmem_limit_bytes=VMEM_LIMIT),
        name="proj_first" if first else "proj",
    )(src, meta_tile, g, w_pack, bf_row, tri, e_mat, qk_bias, lane_rows)


def _attn_kernel(qp_ref, kp_ref, vt_ref, o_ref, sa_ref, sb_ref):
    qi = pl.program_id(2)
    heads = range(HEADS_PER_STEP)
    qs = [qp_ref[0, hd] for hd in heads]

    def scores(j, s_ref, diag=None):
        k0 = pl.multiple_of(j * BK, BK)
        q0 = 0 if diag is None else diag * BK
        for hd in heads:
            s_ref[hd, :, q0:] = lax.dot_general(kp_ref[0, hd, pl.ds(k0, BK), :], qs[hd][q0:],
                                                (((1,), (1,)), ((), ())),
                                                preferred_element_type=jnp.float32)

    def consume(j, s_ref, carries, diag):
        k0 = pl.multiple_of(j * BK, BK)
        q0 = 0 if diag is None else diag * BK
        out = []
        for hd in heads:
            m, acc = (c[:, q0:] for c in carries[hd])
            s = s_ref[hd, :, q0:]
            if diag is not None:
                kpos = lax.broadcasted_iota(jnp.int32, s.shape, 0) + diag * BK
                qpos = lax.broadcasted_iota(jnp.int32, s.shape, 1) + q0
                s = jnp.where(kpos <= qpos, s, -jnp.inf)
            m_new = jnp.maximum(m, jnp.max(s, axis=0, keepdims=True))
            p = jnp.exp2(s - m_new)
            vblk = vt_ref[0, hd, :, pl.ds(k0, BK)]
            acc = jnp.exp2(m - m_new) * acc + jnp.dot(vblk, p.astype(jnp.bfloat16),
                                                      preferred_element_type=jnp.float32)
            new = (m_new, acc)
            if q0:
                new = tuple(jnp.concatenate([c[:, :q0], n], axis=1) for c, n in zip(carries[hd], new))
            out.append(new)
        return tuple(out)

    def finish(carries):
        for hd, (m, acc) in enumerate(carries):
            o_ref[0, hd * HEAD_DIM:(hd + 1) * HEAD_DIM, :] = (acc[0:HEAD_DIM] / acc[HEAD_DIM:HEAD_DIM + 1]
                                                             ).astype(o_ref.dtype)

    init = ((jnp.full((1, BQ), -jnp.inf, jnp.float32), jnp.zeros((VT_ROWS, BQ), jnp.float32)),) * HEADS_PER_STEP

    @pl.when(qi == 0)
    def _():
        scores(0, sa_ref, 0)
        scores(1, sb_ref, 1)
        finish(consume(1, sb_ref, consume(0, sa_ref, init, 0), 1))

    @pl.when(qi > 0)
    def _():
        def pair(t, carries):
            j = 2 * t + 1
            scores(j + 1, sb_ref)
            carries = consume(j, sa_ref, carries, None)
            scores(j + 2, sa_ref)
            return consume(j + 1, sb_ref, carries, None)

        scores(1, sa_ref)
        carries = lax.fori_loop(0, qi - 1, pair, init)
        last = 2 * qi - 1
        scores(last + 1, sb_ref, 0)
        carries = consume(last, sa_ref, carries, None)
        scores(last + 2, sa_ref, 1)
        carries = consume(last + 1, sb_ref, carries, 0)
        finish(consume(last + 2, sa_ref, carries, 1))


def _attn_call(qp, kp, vt):
    bsz, heads, l_pad, _ = qp.shape
    nq = l_pad // BQ
    hps = HEADS_PER_STEP
    return pl.pallas_call(
        _attn_kernel,
        grid=(bsz, heads // hps, nq),
        in_specs=[
            pl.BlockSpec((1, hps, BQ, LANES), lambda b, h, i: (b, h, i, 0)),
            pl.BlockSpec((1, hps, l_pad, LANES), lambda b, h, i: (b, h, 0, 0)),
            pl.BlockSpec((1, hps, VT_ROWS, l_pad), lambda b, h, i: (b, h, 0, 0)),
        ],
        out_specs=pl.BlockSpec((1, hps * HEAD_DIM, BQ), lambda b, h, i: (b, h, i)),
        out_shape=jax.ShapeDtypeStruct((bsz, heads * HEAD_DIM, l_pad), jnp.bfloat16),
        scratch_shapes=[pltpu.VMEM((hps, BK, BQ), jnp.float32), pltpu.VMEM((hps, BK, BQ), jnp.float32)],
        compiler_params=pltpu.CompilerParams(
            dimension_semantics=("arbitrary", "arbitrary", "arbitrary"), vmem_limit_bytes=VMEM_LIMIT),
        name="attn",
    )(qp, kp, vt)


def _causal_taps(ext_ref, w_ref, n_taps, k_lo, k_hi, shift_ref=None):
    out = None
    for k in range(k_lo, k_hi):
        start = HALO - (n_taps - 1) + k
        if shift_ref is None or start % SUBLANES == 0:
            window = ext_ref[pl.ds(start, ROW_TILE), :]
        else:
            window = shift_ref[start % SUBLANES - 1, pl.ds(start - start % SUBLANES, ROW_TILE), :]
        term = window * w_ref[k:k + 1, :]
        out = term if out is None else out + term
    return out


def _zero_of(x):
    bits = pltpu.bitcast(x, jnp.uint32)
    return lax.shift_right_logical(lax.shift_right_logical(bits, jnp.uint32(16)), jnp.uint32(16)).astype(jnp.float32)


def _conv_parts(mixt_ref, halo, wdw_ref, bdw_ref, lng_ref, lnb_ref, wsc_ref, ext_a, ext_c, shift_a, y_ref, sc_ref):
    tap_parts = CONV_PARTS - 2
    bounds = [round(i * CONF_K / tap_parts) for i in range(tap_parts + 1)]
    state = {}

    def first_part(edge):
        ext_a[0:HALO, :] = halo[:, 0:D_CONF] + edge
        ext_a[HALO:, :] = mixt_ref[0, :, 0:D_CONF]
        ext_c[0:HALO, :] = halo[:, D_CONF + D_SC:] + edge
        ext_c[HALO:, :] = mixt_ref[0, :, D_CONF + D_SC:]
        for r in range(1, SUBLANES):
            shift_a[r - 1] = ext_a[pl.ds(r, SHIFT_ROWS), :]
        state["dw"] = bdw_ref[...]

    def middle_part(part, edge):
        state["dw"] = state["dw"] + edge + _causal_taps(ext_a, wdw_ref, CONF_K, bounds[part - 1], bounds[part],
                                                        shift_a)

    def last_part(edge):
        dw = state["dw"] + edge
        mu = jnp.mean(dw, axis=-1, keepdims=True)
        xc = dw - mu
        y = xc * lax.rsqrt(jnp.mean(xc * xc, axis=-1, keepdims=True) + EPS) * lng_ref[...] + lnb_ref[...]
        y = y * jax.nn.sigmoid(y)
        sc = mixt_ref[0, :, D_CONF:D_CONF + D_SC] * _causal_taps(ext_c, wsc_ref, SC_K, 0, SC_K)
        y_ref[...] = y.astype(y_ref.dtype)
        sc_ref[...] = sc.astype(sc_ref.dtype)
        return _zero_of(jnp.max(y, axis=0, keepdims=True) + jnp.max(sc, axis=0, keepdims=True))

    middle = [functools.partial(middle_part, part) for part in range(1, CONV_PARTS - 1)]
    return [first_part] + middle + [last_part]


def _mix_kernel(src_ref, meta_ref, ot_ref, mix_ref, mixn_ref, wdw_ref, bdw_ref, lng_ref, lnb_ref, wpw_ref,
                bpw_ref, wsc_ref, wout_ref, ng_ref, w1_ref, w2_ref, fg_ref, out_ref,
                ext_a, ext_c, shift_a, y_ref, sc_ref, *, first, final):
    conv_parts = functools.partial(_conv_parts, wdw_ref=wdw_ref, bdw_ref=bdw_ref, lng_ref=lng_ref, lnb_ref=lnb_ref,
                                   wsc_ref=wsc_ref, ext_a=ext_a, ext_c=ext_c, shift_a=shift_a,
                                   y_ref=y_ref, sc_ref=sc_ref)

    @pl.when(pl.program_id(1) == 0)
    def _():
        for part in conv_parts(mix_ref, jnp.zeros((HALO, D_CONF + 2 * D_SC), jnp.float32)):
            part(0.0)

    conf = jnp.dot(y_ref[...], wpw_ref[...], preferred_element_type=jnp.float32) + bpw_ref[...]
    sc = sc_ref[...]

    attn = ot_ref[0].T
    mixed = (jnp.dot(attn, wout_ref[0:D_ATTN, :], preferred_element_type=jnp.float32)
             + jnp.dot(conf.astype(jnp.bfloat16), wout_ref[D_ATTN:D_ATTN + D_CONF, :],
                       preferred_element_type=jnp.float32)
             + jnp.dot(sc, wout_ref[D_ATTN + D_CONF:, :], preferred_element_type=jnp.float32))
    h1 = _tile_rows(src_ref, meta_ref, first) + mixed

    next_parts = conv_parts(mixn_ref, mix_ref[0, ROW_TILE - HALO:, :])
    hn = _rms(h1, ng_ref[...]).astype(jnp.bfloat16)
    acc = h1
    for chunk, c0 in enumerate(range(0, D_FF, FF_CHUNK)):
        if chunk == CONV_PARTS:
            hn = hn + jnp.concatenate([conv_zero] * (D_MODEL // D_CONF), axis=1).astype(jnp.bfloat16)
        a = jnp.dot(hn, w1_ref[:, c0:c0 + FF_CHUNK], preferred_element_type=jnp.float32)
        if chunk < CONV_PARTS:
            conv_zero = next_parts[chunk](_zero_of(a[0:1, 0:D_CONF]))
        a = jnp.maximum(a, 0.0)
        acc = acc + jnp.dot((a * a).astype(jnp.bfloat16), w2_ref[c0:c0 + FF_CHUNK, :],
                            preferred_element_type=jnp.float32)
    if final:
        out_ref[0] = _rms(acc, fg_ref[...])
    else:
        out_ref[0] = jnp.where(_pad_rows_below(acc.shape), 0.0, acc)


def _mix_call(src, meta_tile, ot, mixin, wdw, bdw, lng, lnb, wpw, bpw, wsc, wout, ng, w1, w2, fg,
              *, layer, first, final, l_pad):
    bsz = src.shape[0]
    nt = l_pad // ROW_TILE
    const = lambda b, i: (0, 0)
    once = pl.Buffered(1)
    wide = D_CONF + 2 * D_SC
    if final:
        out_spec = pl.BlockSpec((1, ROW_TILE, D_MODEL), lambda b, i: (b, jnp.maximum(i - 1, 0), 0))
        out_shape = jax.ShapeDtypeStruct((bsz, l_pad - ROW_TILE, D_MODEL), jnp.float32)
    else:
        out_spec = pl.BlockSpec((1, ROW_TILE, D_MODEL), lambda b, i: (b, i, 0))
        out_shape = jax.ShapeDtypeStruct((bsz, l_pad, D_MODEL), jnp.float32)
    return pl.pallas_call(
        functools.partial(_mix_kernel, first=first, final=final),
        grid=(bsz, nt),
        in_specs=[
            _src_spec(first),
            pl.BlockSpec((ROW_TILE, D_MODEL), const),
            pl.BlockSpec((1, D_ATTN, ROW_TILE), lambda b, i: (b, 0, i)),
            pl.BlockSpec((1, ROW_TILE, wide), lambda b, i: (b, i, 0)),
            pl.BlockSpec((1, ROW_TILE, wide), lambda b, i: (b, jnp.minimum(i + 1, nt - 1), 0)),
            _layer_spec((HALO, D_CONF), layer),
            _layer_spec((1, D_CONF), layer),
            _layer_spec((1, D_CONF), layer),
            _layer_spec((1, D_CONF), layer),
            _layer_spec((D_CONF, D_CONF), layer),
            _layer_spec((1, D_CONF), layer),
            _layer_spec((SUBLANES, D_SC), layer),
            _layer_spec((D_MODEL, D_MODEL), layer, pipeline_mode=once),
            _layer_spec((1, D_MODEL), layer),
            _layer_spec((D_MODEL, D_FF), layer, pipeline_mode=once),
            _layer_spec((D_FF, D_MODEL), layer, pipeline_mode=once),
            pl.BlockSpec((1, D_MODEL), const),
        ],
        out_specs=out_spec,
        out_shape=out_shape,
        scratch_shapes=[pltpu.VMEM((HALO + ROW_TILE, D_CONF), jnp.float32),
                        pltpu.VMEM((HALO + ROW_TILE, D_SC), jnp.float32),
                        pltpu.VMEM((SUBLANES - 1, SHIFT_ROWS, D_CONF), jnp.float32),
                        pltpu.VMEM((ROW_TILE, D_CONF), jnp.bfloat16),
                        pltpu.VMEM((ROW_TILE, D_SC), jnp.bfloat16)],
        compiler_params=pltpu.CompilerParams(
            dimension_semantics=("arbitrary", "arbitrary"), vmem_limit_bytes=VMEM_LIMIT),
        name="mix_final" if final else "mix",
    )(src, meta_tile, ot, mixin, mixin, wdw, bdw, lng, lnb, wpw, bpw, wsc, wout, ng, w1, w2, fg)


def _decay_selectors():
    assert HEAD_DIM + HEADS * DECAY_LANES <= LANES
    e = np.zeros((N_SPLIT * LANES, 2 * LANES), np.float32)
    bias = np.zeros((1, 2 * LANES), np.float32)
    lane_rows = np.zeros((2 * HEADS, LANES), np.float32)
    for hd in range(HEADS):
        lane0 = HEAD_DIM + hd * DECAY_LANES
        lane_rows[hd, lane0:lane0 + DECAY_LANES] = 1.0
        lane_rows[HEADS, lane0 + N_SPLIT] = PAD_KEY_BIAS
        for j in range(N_SPLIT):
            e[j * LANES + hd, lane0 + j] = 1.0
            bias[0, lane0 + N_SPLIT + j] = 1.0
            bias[0, LANES + lane0 + j] = 1.0
            e[j * LANES + hd, LANES + lane0 + N_SPLIT + j] = -1.0
    return jnp.asarray(e, jnp.bfloat16), jnp.asarray(bias, jnp.float32), jnp.asarray(lane_rows, jnp.float32)


def _pack_w_in(w):
    q = w[..., 0:D_ATTN] * (HEAD_DIM ** -0.5)
    k = w[..., D_ATTN:2 * D_ATTN]
    v = w[..., 2 * D_ATTN:3 * D_ATTN]
    f = jnp.pad(w[..., 3 * D_ATTN:3 * D_ATTN + HEADS], ((0, 0), (0, 0), (0, LANES - HEADS)))
    rest = w[..., 3 * D_ATTN + HEADS:]
    return jnp.concatenate([f, q, k, v, rest], -1).astype(jnp.bfloat16)


def kernel(x, meta_tokens, mix_norm_g, w_in, b_forget, w_conf_dw, b_conf_dw, conf_ln_g, conf_ln_b,
           w_conf_pw, b_conf_pw, w_sc_conv, w_out, mlp_norm_g, w_mlp1, w_mlp2, final_norm_g):
    bsz, seq, d = x.shape
    depth = w_in.shape[0]
    assert seq % ROW_TILE == 0 and meta_tokens.shape[0] == N_META and PAD_FRONT >= BK and BQ == 2 * BK
    l_pad = ROW_TILE + seq

    meta_tile = jnp.concatenate([jnp.zeros((PAD_FRONT, d), x.dtype), meta_tokens.astype(x.dtype)], axis=0)
    tri = jnp.asarray(np.tril(np.ones((ROW_TILE, ROW_TILE), np.float32)), jnp.bfloat16)
    e_mat, qk_bias, lane_rows = _decay_selectors()

    rows = lambda a: a[:, None, :].astype(jnp.float32)
    bf16 = lambda a: a.astype(jnp.bfloat16)
    w_pack, bf_rows = _pack_w_in(w_in), jnp.pad(rows(b_forget), ((0, 0), (0, 0), (0, LANES - HEADS)))
    wdw = jnp.pad(w_conf_dw, ((0, 0), (0, HALO - CONF_K), (0, 0)))
    wsc = jnp.pad(w_sc_conv, ((0, 0), (0, SUBLANES - SC_K), (0, 0)))
    mix_params = (wdw, rows(b_conf_dw), rows(conf_ln_g), rows(conf_ln_b), bf16(w_conf_pw), rows(b_conf_pw), wsc,
                  bf16(w_out), rows(mlp_norm_g), bf16(w_mlp1), bf16(w_mlp2),
                  final_norm_g.reshape(1, -1).astype(jnp.float32))
    g_rows = rows(mix_norm_g)

    h = x
    for l in range(depth):
        first, final = l == 0, l == depth - 1
        qp, kp, vt, mixin = _proj_call(h, meta_tile, g_rows, w_pack, bf_rows, tri, e_mat, qk_bias, lane_rows,
                                       layer=l, first=first, l_pad=l_pad)
        ot = _attn_call(qp, kp, vt)
        h = _mix_call(h, meta_tile, ot, mixin, *mix_params, layer=l, first=first, final=final, l_pad=l_pad)
    return h
```

```python
import functools

import jax
import jax.numpy as jnp
import numpy as np
from jax import lax
from jax.experimental import pallas as pl
from jax.experimental.pallas import tpu as pltpu

D_MODEL = 1024
N_META = 16
HEADS = 8
HEAD_DIM = 64
D_ATTN = HEADS * HEAD_DIM
D_CONF = 256
D_SC = 256
CONF_K = 31
SC_K = 3
D_FF = 4 * D_MODEL
EPS = 1e-6

LANES = 128
SUBLANES = 8
ROW_TILE = 512
PAD_FRONT = ROW_TILE - N_META
BQ = ROW_TILE
BK = 256
HEADS_PER_STEP = 2
LOG2E = 1.4426950408889634
VT_ROWS = HEAD_DIM + 16
PAD_KEY_BIAS = -2.0 ** 100
HALO = 32
SHIFT_ROWS = ROW_TILE + HALO - SUBLANES
FF_CHUNK = 512
CONV_PARTS = 7
N_SPLIT = 3
VMEM_LIMIT = 56 * 1024 * 1024

C_F = 0
C_Q = C_F + LANES
C_K = C_Q + D_ATTN
C_V = C_K + D_ATTN
C_REST = C_V + D_ATTN
N_PACK = C_REST + 2 * D_CONF + 3 * D_SC
DECAY_LANES = 2 * N_SPLIT


def _split_bf16(x):
    pieces = []
    r = x
    for _ in range(N_SPLIT):
        p = r.astype(jnp.bfloat16)
        pieces.append(p)
        r = r - p.astype(jnp.float32)
    return pieces


def _rms(x, g):
    return x * lax.rsqrt(jnp.mean(x * x, axis=-1, keepdims=True) + EPS) * g


def _tile_rows(src_ref, meta_ref, first):
    h = src_ref[0]
    if first:
        h = jnp.where(pl.program_id(1) == 0, meta_ref[...], h)
    return h


def _pad_rows_below(shape):
    limit = jnp.where(pl.program_id(1) == 0, PAD_FRONT, 0)
    return lax.broadcasted_iota(jnp.int32, shape, 0) < limit


def _proj_kernel(src_ref, meta_ref, g_ref, w_ref, bf_ref, tri_ref, e_ref, qkb_ref, lanes_ref,
                 qp_ref, kp_ref, vt_ref, mix_ref, carry_ref, *, first):
    @pl.when(pl.program_id(1) == 0)
    def _():
        carry_ref[...] = jnp.zeros_like(carry_ref)

    hn = _rms(_tile_rows(src_ref, meta_ref, first), g_ref[...]).astype(jnp.bfloat16)

    fq = jnp.dot(hn, w_ref[:, C_F:C_K], preferred_element_type=jnp.float32)
    z = fq[:, :LANES] + bf_ref[...]
    log_f = jnp.minimum(z, 0.0) - jnp.log1p(jnp.exp(-jnp.abs(z)))
    sums = jnp.dot(tri_ref[...], jnp.concatenate(_split_bf16(log_f), axis=-1),
                   preferred_element_type=jnp.float32)
    c = carry_ref[0:1, :]
    for piece in range(N_SPLIT):
        c = c + sums[:, piece * LANES:(piece + 1) * LANES]
    carry_ref[0:1, :] = c[ROW_TILE - 1:ROW_TILE, :]

    c_pieces = jnp.concatenate(_split_bf16(c * LOG2E), axis=-1)
    decay = jnp.dot(c_pieces, e_ref[...], preferred_element_type=jnp.float32) + qkb_ref[...]
    decay_q, decay_k = decay[:, :LANES], decay[:, LANES:]
    decay_k = decay_k + jnp.where(_pad_rows_below((ROW_TILE, LANES)), lanes_ref[HEADS:HEADS + 1, :], 0.0)
    q = fq[:, LANES:] * LOG2E
    k = jnp.dot(hn, w_ref[:, C_K:C_V], preferred_element_type=jnp.float32)
    low_half = lax.broadcasted_iota(jnp.int32, (ROW_TILE, LANES), 1) < HEAD_DIM
    for hd in range(HEADS):
        pair_cols = slice((hd // 2) * LANES, (hd // 2 + 1) * LANES)
        q_h, k_h = q[:, pair_cols], k[:, pair_cols]
        if hd % 2:
            q_h, k_h = pltpu.roll(q_h, HEAD_DIM, axis=1), pltpu.roll(k_h, HEAD_DIM, axis=1)
        qp_ref[0, hd] = jnp.where(low_half, q_h, decay_q).astype(jnp.bfloat16)
        kp_ref[0, hd] = jnp.where(low_half, k_h, decay_k * lanes_ref[hd:hd + 1, :]).astype(jnp.bfloat16)

    v = jnp.dot(hn, w_ref[:, C_V:C_REST], preferred_element_type=jnp.float32)
    vt = v.T.astype(jnp.bfloat16)
    ones_row = (lax.broadcasted_iota(jnp.int32, (VT_ROWS - HEAD_DIM, ROW_TILE), 0) == 0).astype(jnp.bfloat16)
    for hd in range(HEADS):
        vt_ref[0, hd, 0:HEAD_DIM, :] = vt[hd * HEAD_DIM:(hd + 1) * HEAD_DIM, :]
        vt_ref[0, hd, HEAD_DIM:, :] = ones_row

    r = jnp.dot(hn, w_ref[:, C_REST:N_PACK], preferred_element_type=jnp.float32)
    conf_a = r[:, 0:D_CONF]
    conf_g = r[:, D_CONF:2 * D_CONF]
    sc_b = r[:, 2 * D_CONF:2 * D_CONF + D_SC]
    sc_c = r[:, 2 * D_CONF + D_SC:2 * D_CONF + 2 * D_SC]
    sc_u = r[:, 2 * D_CONF + 2 * D_SC:]
    mix_ref[0, :, 0:D_CONF] = conf_a * jax.nn.sigmoid(conf_g)
    mix_ref[0, :, D_CONF:D_CONF + D_SC] = sc_b
    mix_ref[0, :, D_CONF + D_SC:] = sc_c * sc_u


def _src_spec(first):
    if first:
        return pl.BlockSpec((1, ROW_TILE, D_MODEL), lambda b, i: (b, jnp.maximum(i - 1, 0), 0))
    return pl.BlockSpec((1, ROW_TILE, D_MODEL), lambda b, i: (b, i, 0))


def _layer_spec(shape, layer, **kwargs):
    return pl.BlockSpec((None,) + shape, lambda b, i: (layer,) + (0,) * len(shape), **kwargs)


def _proj_call(src, meta_tile, g, w_pack, bf_row, tri, e_mat, qk_bias, lane_rows, *, layer, first, l_pad):
    bsz = src.shape[0]
    nt = l_pad // ROW_TILE
    const = lambda b, i: (0, 0)
    return pl.pallas_call(
        functools.partial(_proj_kernel, first=first),
        grid=(bsz, nt),
        in_specs=[
            _src_spec(first),
            pl.BlockSpec((ROW_TILE, D_MODEL), const),
            _layer_spec((1, D_MODEL), layer),
            _layer_spec((D_MODEL, N_PACK), layer),
            _layer_spec((1, LANES), layer),
            pl.BlockSpec((ROW_TILE, ROW_TILE), const),
            pl.BlockSpec((N_SPLIT * LANES, 2 * LANES), const),
            pl.BlockSpec((1, 2 * LANES), const),
            pl.BlockSpec((2 * HEADS, LANES), const),
        ],
        out_specs=[
            pl.BlockSpec((1, HEADS, ROW_TILE, LANES), lambda b, i: (b, 0, i, 0)),
            pl.BlockSpec((1, HEADS, ROW_TILE, LANES), lambda b, i: (b, 0, i, 0)),
            pl.BlockSpec((1, HEADS, VT_ROWS, ROW_TILE), lambda b, i: (b, 0, 0, i)),
            pl.BlockSpec((1, ROW_TILE, D_CONF + 2 * D_SC), lambda b, i: (b, i, 0)),
        ],
        out_shape=[
            jax.ShapeDtypeStruct((bsz, HEADS, l_pad, LANES), jnp.bfloat16),
            jax.ShapeDtypeStruct((bsz, HEADS, l_pad, LANES), jnp.bfloat16),
            jax.ShapeDtypeStruct((bsz, HEADS, VT_ROWS, l_pad), jnp.bfloat16),
            jax.ShapeDtypeStruct((bsz, l_pad, D_CONF + 2 * D_SC), jnp.float32),
        ],
        scratch_shapes=[pltpu.VMEM((8, LANES), jnp.float32)],
        compiler_params=pltpu.CompilerParams(
            dimension_semantics=("arbitrary", "arbitrary"), vmem_limit_bytes=VMEM_LIMIT),
        name="proj_first" if first else "proj",
    )(src, meta_tile, g, w_pack, bf_row, tri, e_mat, qk_bias, lane_rows)


def _attn_kernel(qp_ref, kp_ref, vt_ref, o_ref, sa_ref, sb_ref):
    qi = pl.program_id(2)
    heads = range(HEADS_PER_STEP)
    qs = [qp_ref[0, hd] for hd in heads]

    def scores(j, s_ref, diag=None):
        k0 = pl.multiple_of(j * BK, BK)
        q0 = 0 if diag is None else diag * BK
        for hd in heads:
            s_ref[hd, :, q0:] = lax.dot_general(kp_ref[0, hd, pl.ds(k0, BK), :], qs[hd][q0:],
                                                (((1,), (1,)), ((), ())),
                                                preferred_element_type=jnp.float32)

    def consume(j, s_ref, carries, diag):
        k0 = pl.multiple_of(j * BK, BK)
        q0 = 0 if diag is None else diag * BK
        out = []
        for hd in heads:
            m, acc = (c[:, q0:] for c in carries[hd])
            s = s_ref[hd, :, q0:]
            if diag is not None:
                kpos = lax.broadcasted_iota(jnp.int32, s.shape, 0) + diag * BK
                qpos = lax.broadcasted_iota(jnp.int32, s.shape, 1) + q0
                s = jnp.where(kpos <= qpos, s, -jnp.inf)
            m_new = jnp.maximum(m, jnp.max(s, axis=0, keepdims=True))
            p = jnp.exp2(s - m_new)
            vblk = vt_ref[0, hd, :, pl.ds(k0, BK)]
            acc = jnp.exp2(m - m_new) * acc + jnp.dot(vblk, p.astype(jnp.bfloat16),
                                                      preferred_element_type=jnp.float32)
            new = (m_new, acc)
            if q0:
                new = tuple(jnp.concatenate([c[:, :q0], n], axis=1) for c, n in zip(carries[hd], new))
            out.append(new)
        return tuple(out)

    def finish(carries):
        for hd, (m, acc) in enumerate(carries):
            o_ref[0, hd * HEAD_DIM:(hd + 1) * HEAD_DIM, :] = (acc[0:HEAD_DIM] / acc[HEAD_DIM:HEAD_DIM + 1]
                                                             ).astype(o_ref.dtype)

    init = ((jnp.full((1, BQ), -jnp.inf, jnp.float32), jnp.zeros((VT_ROWS, BQ), jnp.float32)),) * HEADS_PER_STEP

    @pl.when(qi == 0)
    def _():
        scores(0, sa_ref, 0)
        scores(1, sb_ref, 1)
        finish(consume(1, sb_ref, consume(0, sa_ref, init, 0), 1))

    @pl.when(qi > 0)
    def _():
        def pair(t, carries):
            j = 2 * t + 1
            scores(j + 1, sb_ref)
            carries = consume(j, sa_ref, carries, None)
            scores(j + 2, sa_ref)
            return consume(j + 1, sb_ref, carries, None)

        scores(1, sa_ref)
        carries = lax.fori_loop(0, (qi - 1) // 2, lambda t, cr: pair(2 * t + 1, pair(2 * t, cr)), init)
        carries = lax.cond((qi - 1) % 2 == 1, lambda cr: pair(qi - 2, cr), lambda cr: cr, carries)
        last = 2 * qi - 1
        scores(last + 1, sb_ref, 0)
        carries = consume(last, sa_ref, carries, None)
        scores(last + 2, sa_ref, 1)
        carries = consume(last + 1, sb_ref, carries, 0)
        finish(consume(last + 2, sa_ref, carries, 1))


def _attn_call(qp, kp, vt):
    bsz, heads, l_pad, _ = qp.shape
    nq = l_pad // BQ
    hps = HEADS_PER_STEP
    return pl.pallas_call(
        _attn_kernel,
        grid=(bsz, heads // hps, nq),
        in_specs=[
            pl.BlockSpec((1, hps, BQ, LANES), lambda b, h, i: (b, h, i, 0)),
            pl.BlockSpec((1, hps, l_pad, LANES), lambda b, h, i: (b, h, 0, 0)),
            pl.BlockSpec((1, hps, VT_ROWS, l_pad), lambda b, h, i: (b, h, 0, 0)),
        ],
        out_specs=pl.BlockSpec((1, hps * HEAD_DIM, BQ), lambda b, h, i: (b, h, i)),
        out_shape=jax.ShapeDtypeStruct((bsz, heads * HEAD_DIM, l_pad), jnp.bfloat16),
        scratch_shapes=[pltpu.VMEM((hps, BK, BQ), jnp.float32), pltpu.VMEM((hps, BK, BQ), jnp.float32)],
        compiler_params=pltpu.CompilerParams(
            dimension_semantics=("arbitrary", "arbitrary", "arbitrary"), vmem_limit_bytes=VMEM_LIMIT),
        name="attn",
    )(qp, kp, vt)


def _causal_taps(ext_ref, w_ref, n_taps, k_lo, k_hi, shift_ref=None):
    out = None
    for k in range(k_lo, k_hi):
        start = HALO - (n_taps - 1) + k
        if shift_ref is None or start % SUBLANES == 0:
            window = ext_ref[pl.ds(start, ROW_TILE), :]
        else:
            window = shift_ref[start % SUBLANES - 1, pl.ds(start - start % SUBLANES, ROW_TILE), :]
        term = window * w_ref[k:k + 1, :]
        out = term if out is None else out + term
    return out


def _zero_of(x):
    bits = pltpu.bitcast(x, jnp.uint32)
    return lax.shift_right_logical(lax.shift_right_logical(bits, jnp.uint32(16)), jnp.uint32(16)).astype(jnp.float32)


def _conv_parts(mixt_ref, halo, wdw_ref, bdw_ref, lng_ref, lnb_ref, wsc_ref, ext_a, ext_c, shift_a, y_ref, sc_ref):
    tap_parts = CONV_PARTS - 2
    bounds = [round(i * CONF_K / tap_parts) for i in range(tap_parts + 1)]
    state = {}

    def first_part(edge):
        ext_a[0:HALO, :] = halo[:, 0:D_CONF] + edge
        ext_a[HALO:, :] = mixt_ref[0, :, 0:D_CONF]
        ext_c[0:HALO, :] = halo[:, D_CONF + D_SC:] + edge
        ext_c[HALO:, :] = mixt_ref[0, :, D_CONF + D_SC:]
        for r in range(1, SUBLANES):
            shift_a[r - 1] = ext_a[pl.ds(r, SHIFT_ROWS), :]
        state["dw"] = bdw_ref[...]

    def middle_part(part, edge):
        state["dw"] = state["dw"] + edge + _causal_taps(ext_a, wdw_ref, CONF_K, bounds[part - 1], bounds[part],
                                                        shift_a)

    def last_part(edge):
        dw = state["dw"] + edge
        mu = jnp.mean(dw, axis=-1, keepdims=True)
        xc = dw - mu
        y = xc * lax.rsqrt(jnp.mean(xc * xc, axis=-1, keepdims=True) + EPS) * lng_ref[...] + lnb_ref[...]
        y = y * jax.nn.sigmoid(y)
        sc = mixt_ref[0, :, D_CONF:D_CONF + D_SC] * _causal_taps(ext_c, wsc_ref, SC_K, 0, SC_K)
        y_ref[...] = y.astype(y_ref.dtype)
        sc_ref[...] = sc.astype(sc_ref.dtype)
        return _zero_of(jnp.max(y, axis=0, keepdims=True) + jnp.max(sc, axis=0, keepdims=True))

    middle = [functools.partial(middle_part, part) for part in range(1, CONV_PARTS - 1)]
    return [first_part] + middle + [last_part]


def _mix_kernel(src_ref, meta_ref, ot_ref, mix_ref, mixn_ref, wdw_ref, bdw_ref, lng_ref, lnb_ref, wpw_ref,
                bpw_ref, wsc_ref, wout_ref, ng_ref, w1_ref, w2_ref, fg_ref, out_ref,
                ext_a, ext_c, shift_a, y_ref, sc_ref, *, first, final):
    conv_parts = functools.partial(_conv_parts, wdw_ref=wdw_ref, bdw_ref=bdw_ref, lng_ref=lng_ref, lnb_ref=lnb_ref,
                                   wsc_ref=wsc_ref, ext_a=ext_a, ext_c=ext_c, shift_a=shift_a,
                                   y_ref=y_ref, sc_ref=sc_ref)

    @pl.when(pl.program_id(1) == 0)
    def _():
        for part in conv_parts(mix_ref, jnp.zeros((HALO, D_CONF + 2 * D_SC), jnp.float32)):
            part(0.0)

    conf = jnp.dot(y_ref[...], wpw_ref[...], preferred_element_type=jnp.float32) + bpw_ref[...]
    sc = sc_ref[...]

    attn = ot_ref[0].T
    mixed = (jnp.dot(attn, wout_ref[0:D_ATTN, :], preferred_element_type=jnp.float32)
             + jnp.dot(conf.astype(jnp.bfloat16), wout_ref[D_ATTN:D_ATTN + D_CONF, :],
                       preferred_element_type=jnp.float32)
             + jnp.dot(sc, wout_ref[D_ATTN + D_CONF:, :], preferred_element_type=jnp.float32))
    h1 = _tile_rows(src_ref, meta_ref, first) + mixed

    next_parts = conv_parts(mixn_ref, mix_ref[0, ROW_TILE - HALO:, :])
    hn = _rms(h1, ng_ref[...]).astype(jnp.bfloat16)
    acc = h1
    for chunk, c0 in enumerate(range(0, D_FF, FF_CHUNK)):
        if chunk == CONV_PARTS:
            hn = hn + jnp.concatenate([conv_zero] * (D_MODEL // D_CONF), axis=1).astype(jnp.bfloat16)
        a = jnp.dot(hn, w1_ref[:, c0:c0 + FF_CHUNK], preferred_element_type=jnp.float32)
        if chunk < CONV_PARTS:
            conv_zero = next_parts[chunk](_zero_of(a[0:1, 0:D_CONF]))
        a = jnp.maximum(a, 0.0)
        acc = acc + jnp.dot((a * a).astype(jnp.bfloat16), w2_ref[c0:c0 + FF_CHUNK, :],
                            preferred_element_type=jnp.float32)
    if final:
        out_ref[0] = _rms(acc, fg_ref[...])
    else:
        out_ref[0] = jnp.where(_pad_rows_below(acc.shape), 0.0, acc)


def _mix_call(src, meta_tile, ot, mixin, wdw, bdw, lng, lnb, wpw, bpw, wsc, wout, ng, w1, w2, fg,
              *, layer, first, final, l_pad):
    bsz = src.shape[0]
    nt = l_pad // ROW_TILE
    const = lambda b, i: (0, 0)
    once = pl.Buffered(1)
    wide = D_CONF + 2 * D_SC
    if final:
        out_spec = pl.BlockSpec((1, ROW_TILE, D_MODEL), lambda b, i: (b, jnp.maximum(i - 1, 0), 0))
        out_shape = jax.ShapeDtypeStruct((bsz, l_pad - ROW_TILE, D_MODEL), jnp.float32)
    else:
        out_spec = pl.BlockSpec((1, ROW_TILE, D_MODEL), lambda b, i: (b, i, 0))
        out_shape = jax.ShapeDtypeStruct((bsz, l_pad, D_MODEL), jnp.float32)
    return pl.pallas_call(
        functools.partial(_mix_kernel, first=first, final=final),
        grid=(bsz, nt),
        in_specs=[
            _src_spec(first),
            pl.BlockSpec((ROW_TILE, D_MODEL), const),
            pl.BlockSpec((1, D_ATTN, ROW_TILE), lambda b, i: (b, 0, i)),
            pl.BlockSpec((1, ROW_TILE, wide), lambda b, i: (b, i, 0)),
            pl.BlockSpec((1, ROW_TILE, wide), lambda b, i: (b, jnp.minimum(i + 1, nt - 1), 0)),
            _layer_spec((HALO, D_CONF), layer),
            _layer_spec((1, D_CONF), layer),
            _layer_spec((1, D_CONF), layer),
            _layer_spec((1, D_CONF), layer),
            _layer_spec((D_CONF, D_CONF), layer),
            _layer_spec((1, D_CONF), layer),
            _layer_spec((SUBLANES, D_SC), layer),
            _layer_spec((D_MODEL, D_MODEL), layer, pipeline_mode=once),
            _layer_spec((1, D_MODEL), layer),
            _layer_spec((D_MODEL, D_FF), layer, pipeline_mode=once),
            _layer_spec((D_FF, D_MODEL), layer, pipeline_mode=once),
            pl.BlockSpec((1, D_MODEL), const),
        ],
        out_specs=out_spec,
        out_shape=out_shape,
        scratch_shapes=[pltpu.VMEM((HALO + ROW_TILE, D_CONF), jnp.float32),
                        pltpu.VMEM((HALO + ROW_TILE, D_SC), jnp.float32),
                        pltpu.VMEM((SUBLANES - 1, SHIFT_ROWS, D_CONF), jnp.float32),
                        pltpu.VMEM((ROW_TILE, D_CONF), jnp.bfloat16),
                        pltpu.VMEM((ROW_TILE, D_SC), jnp.bfloat16)],
        compiler_params=pltpu.CompilerParams(
            dimension_semantics=("arbitrary", "arbitrary"), vmem_limit_bytes=VMEM_LIMIT),
        name="mix_final" if final else "mix",
    )(src, meta_tile, ot, mixin, mixin, wdw, bdw, lng, lnb, wpw, bpw, wsc, wout, ng, w1, w2, fg)


def _decay_selectors():
    assert HEAD_DIM + HEADS * DECAY_LANES <= LANES
    e = np.zeros((N_SPLIT * LANES, 2 * LANES), np.float32)
    bias = np.zeros((1, 2 * LANES), np.float32)
    lane_rows = np.zeros((2 * HEADS, LANES), np.float32)
    for hd in range(HEADS):
        lane0 = HEAD_DIM + hd * DECAY_LANES
        lane_rows[hd, lane0:lane0 + DECAY_LANES] = 1.0
        lane_rows[HEADS, lane0 + N_SPLIT] = PAD_KEY_BIAS
        for j in range(N_SPLIT):
            e[j * LANES + hd, lane0 + j] = 1.0
            bias[0, lane0 + N_SPLIT + j] = 1.0
            bias[0, LANES + lane0 + j] = 1.0
            e[j * LANES + hd, LANES + lane0 + N_SPLIT + j] = -1.0
    return jnp.asarray(e, jnp.bfloat16), jnp.asarray(bias, jnp.float32), jnp.asarray(lane_rows, jnp.float32)


def _pack_w_in(w):
    q = w[..., 0:D_ATTN] * (HEAD_DIM ** -0.5)
    k = w[..., D_ATTN:2 * D_ATTN]
    v = w[..., 2 * D_ATTN:3 * D_ATTN]
    f = jnp.pad(w[..., 3 * D_ATTN:3 * D_ATTN + HEADS], ((0, 0), (0, 0), (0, LANES - HEADS)))
    rest = w[..., 3 * D_ATTN + HEADS:]
    return jnp.concatenate([f, q, k, v, rest], -1).astype(jnp.bfloat16)


def kernel(x, meta_tokens, mix_norm_g, w_in, b_forget, w_conf_dw, b_conf_dw, conf_ln_g, conf_ln_b,
           w_conf_pw, b_conf_pw, w_sc_conv, w_out, mlp_norm_g, w_mlp1, w_mlp2, final_norm_g):
    bsz, seq, d = x.shape
    depth = w_in.shape[0]
    assert seq % ROW_TILE == 0 and meta_tokens.shape[0] == N_META and PAD_FRONT >= BK and BQ == 2 * BK
    l_pad = ROW_TILE + seq

    meta_tile = jnp.concatenate([jnp.zeros((PAD_FRONT, d), x.dtype), meta_tokens.astype(x.dtype)], axis=0)
    tri = jnp.asarray(np.tril(np.ones((ROW_TILE, ROW_TILE), np.float32)), jnp.bfloat16)
    e_mat, qk_bias, lane_rows = _decay_selectors()

    rows = lambda a: a[:, None, :].astype(jnp.float32)
    bf16 = lambda a: a.astype(jnp.bfloat16)
    w_pack, bf_rows = _pack_w_in(w_in), jnp.pad(rows(b_forget), ((0, 0), (0, 0), (0, LANES - HEADS)))
    wdw = jnp.pad(w_conf_dw, ((0, 0), (0, HALO - CONF_K), (0, 0)))
    wsc = jnp.pad(w_sc_conv, ((0, 0), (0, SUBLANES - SC_K), (0, 0)))
    mix_params = (wdw, rows(b_conf_dw), rows(conf_ln_g), rows(conf_ln_b), bf16(w_conf_pw), rows(b_conf_pw), wsc,
                  bf16(w_out), rows(mlp_norm_g), bf16(w_mlp1), bf16(w_mlp2),
                  final_norm_g.reshape(1, -1).astype(jnp.float32))
    g_rows = rows(mix_norm_g)

    h = x
    for l in range(depth):
        first, final = l == 0, l == depth - 1
        qp, kp, vt, mixin = _proj_call(h, meta_tile, g_rows, w_pack, bf_rows, tri, e_mat, qk_bias, lane_rows,
                                       layer=l, first=first, l_pad=l_pad)
        ot = _attn_call(qp, kp, vt)
        h = _mix_call(h, meta_tile, ot, mixin, *mix_params, layer=l, first=first, final=final, l_pad=l_pad)
    return h
```

```python
import functools

import jax
import jax.numpy as jnp
import numpy as np
from jax import lax
from jax.experimental import pallas as pl
from jax.experimental.pallas import tpu as pltpu

D_MODEL = 1024
N_META = 16
HEADS = 8
HEAD_DIM = 64
D_ATTN = HEADS * HEAD_DIM
D_CONF = 256
D_SC = 256
CONF_K = 31
SC_K = 3
D_FF = 4 * D_MODEL
EPS = 1e-6

LANES = 128
SUBLANES = 8
ROW_TILE = 512
PAD_FRONT = ROW_TILE - N_META
BQ = ROW_TILE
BK = 256
HEADS_PER_STEP = 2
PAIRS_PER_TRIP = 4
LOG2E = 1.4426950408889634
VT_ROWS = HEAD_DIM + 16
PAD_KEY_BIAS = -2.0 ** 100
HALO = 32
SHIFT_ROWS = ROW_TILE + HALO - SUBLANES
FF_CHUNK = 512
CONV_PARTS = 7
N_SPLIT = 3
VMEM_LIMIT = 56 * 1024 * 1024

C_F = 0
C_Q = C_F + LANES
C_K = C_Q + D_ATTN
C_V = C_K + D_ATTN
C_REST = C_V + D_ATTN
N_PACK = C_REST + 2 * D_CONF + 3 * D_SC
DECAY_LANES = 2 * N_SPLIT


def _split_bf16(x):
    pieces = []
    r = x
    for _ in range(N_SPLIT):
        p = r.astype(jnp.bfloat16)
        pieces.append(p)
        r = r - p.astype(jnp.float32)
    return pieces


def _rms(x, g):
    return x * lax.rsqrt(jnp.mean(x * x, axis=-1, keepdims=True) + EPS) * g


def _tile_rows(src_ref, meta_ref, first):
    h = src_ref[0]
    if first:
        h = jnp.where(pl.program_id(1) == 0, meta_ref[...], h)
    return h


def _pad_rows_below(shape):
    limit = jnp.where(pl.program_id(1) == 0, PAD_FRONT, 0)
    return lax.broadcasted_iota(jnp.int32, shape, 0) < limit


def _proj_kernel(src_ref, meta_ref, g_ref, w_ref, bf_ref, tri_ref, e_ref, qkb_ref, lanes_ref,
                 qp_ref, kp_ref, vt_ref, mix_ref, carry_ref, *, first):
    @pl.when(pl.program_id(1) == 0)
    def _():
        carry_ref[...] = jnp.zeros_like(carry_ref)

    hn = _rms(_tile_rows(src_ref, meta_ref, first), g_ref[...]).astype(jnp.bfloat16)

    fq = jnp.dot(hn, w_ref[:, C_F:C_K], preferred_element_type=jnp.float32)
    z = fq[:, :LANES] + bf_ref[...]
    log_f = jnp.minimum(z, 0.0) - jnp.log1p(jnp.exp(-jnp.abs(z)))
    sums = jnp.dot(tri_ref[...], jnp.concatenate(_split_bf16(log_f), axis=-1),
                   preferred_element_type=jnp.float32)
    c = carry_ref[0:1, :]
    for piece in range(N_SPLIT):
        c = c + sums[:, piece * LANES:(piece + 1) * LANES]
    carry_ref[0:1, :] = c[ROW_TILE - 1:ROW_TILE, :]

    c_pieces = jnp.concatenate(_split_bf16(c * LOG2E), axis=-1)
    decay = jnp.dot(c_pieces, e_ref[...], preferred_element_type=jnp.float32) + qkb_ref[...]
    decay_q, decay_k = decay[:, :LANES], decay[:, LANES:]
    decay_k = decay_k + jnp.where(_pad_rows_below((ROW_TILE, LANES)), lanes_ref[HEADS:HEADS + 1, :], 0.0)
    q = fq[:, LANES:] * LOG2E
    k = jnp.dot(hn, w_ref[:, C_K:C_V], preferred_element_type=jnp.float32)
    low_half = lax.broadcasted_iota(jnp.int32, (ROW_TILE, LANES), 1) < HEAD_DIM
    for hd in range(HEADS):
        pair_cols = slice((hd // 2) * LANES, (hd // 2 + 1) * LANES)
        q_h, k_h = q[:, pair_cols], k[:, pair_cols]
        if hd % 2:
            q_h, k_h = pltpu.roll(q_h, HEAD_DIM, axis=1), pltpu.roll(k_h, HEAD_DIM, axis=1)
        qp_ref[0, hd] = jnp.where(low_half, q_h, decay_q).astype(jnp.bfloat16)
        kp_ref[0, hd] = jnp.where(low_half, k_h, decay_k * lanes_ref[hd:hd + 1, :]).astype(jnp.bfloat16)

    v = jnp.dot(hn, w_ref[:, C_V:C_REST], preferred_element_type=jnp.float32)
    vt = v.T.astype(jnp.bfloat16)
    ones_row = (lax.broadcasted_iota(jnp.int32, (VT_ROWS - HEAD_DIM, ROW_TILE), 0) == 0).astype(jnp.bfloat16)
    for hd in range(HEADS):
        vt_ref[0, hd, 0:HEAD_DIM, :] = vt[hd * HEAD_DIM:(hd + 1) * HEAD_DIM, :]
        vt_ref[0, hd, HEAD_DIM:, :] = ones_row

    r = jnp.dot(hn, w_ref[:, C_REST:N_PACK], preferred_element_type=jnp.float32)
    conf_a = r[:, 0:D_CONF]
    conf_g = r[:, D_CONF:2 * D_CONF]
    sc_b = r[:, 2 * D_CONF:2 * D_CONF + D_SC]
    sc_c = r[:, 2 * D_CONF + D_SC:2 * D_CONF + 2 * D_SC]
    sc_u = r[:, 2 * D_CONF + 2 * D_SC:]
    mix_ref[0, :, 0:D_CONF] = conf_a * jax.nn.sigmoid(conf_g)
    mix_ref[0, :, D_CONF:D_CONF + D_SC] = sc_b
    mix_ref[0, :, D_CONF + D_SC:] = sc_c * sc_u


def _src_spec(first):
    if first:
        return pl.BlockSpec((1, ROW_TILE, D_MODEL), lambda b, i: (b, jnp.maximum(i - 1, 0), 0))
    return pl.BlockSpec((1, ROW_TILE, D_MODEL), lambda b, i: (b, i, 0))


def _layer_spec(shape, layer, **kwargs):
    return pl.BlockSpec((None,) + shape, lambda b, i: (layer,) + (0,) * len(shape), **kwargs)


def _proj_call(src, meta_tile, g, w_pack, bf_row, tri, e_mat, qk_bias, lane_rows, *, layer, first, l_pad):
    bsz = src.shape[0]
    nt = l_pad // ROW_TILE
    const = lambda b, i: (0, 0)
    return pl.pallas_call(
        functools.partial(_proj_kernel, first=first),
        grid=(bsz, nt),
        in_specs=[
            _src_spec(first),
            pl.BlockSpec((ROW_TILE, D_MODEL), const),
            _layer_spec((1, D_MODEL), layer),
            _layer_spec((D_MODEL, N_PACK), layer),
            _layer_spec((1, LANES), layer),
            pl.BlockSpec((ROW_TILE, ROW_TILE), const),
            pl.BlockSpec((N_SPLIT * LANES, 2 * LANES), const),
            pl.BlockSpec((1, 2 * LANES), const),
            pl.BlockSpec((2 * HEADS, LANES), const),
        ],
        out_specs=[
            pl.BlockSpec((1, HEADS, ROW_TILE, LANES), lambda b, i: (b, 0, i, 0)),
            pl.BlockSpec((1, HEADS, ROW_TILE, LANES), lambda b, i: (b, 0, i, 0)),
            pl.BlockSpec((1, HEADS, VT_ROWS, ROW_TILE), lambda b, i: (b, 0, 0, i)),
            pl.BlockSpec((1, ROW_TILE, D_CONF + 2 * D_SC), lambda b, i: (b, i, 0)),
        ],
        out_shape=[
            jax.ShapeDtypeStruct((bsz, HEADS, l_pad, LANES), jnp.bfloat16),
            jax.ShapeDtypeStruct((bsz, HEADS, l_pad, LANES), jnp.bfloat16),
            jax.ShapeDtypeStruct((bsz, HEADS, VT_ROWS, l_pad), jnp.bfloat16),
            jax.ShapeDtypeStruct((bsz, l_pad, D_CONF + 2 * D_SC), jnp.float32),
        ],
        scratch_shapes=[pltpu.VMEM((8, LANES), jnp.float32)],
        compiler_params=pltpu.CompilerParams(
            dimension_semantics=("arbitrary", "arbitrary"), vmem_limit_bytes=VMEM_LIMIT),
        name="proj_first" if first else "proj",
    )(src, meta_tile, g, w_pack, bf_row, tri, e_mat, qk_bias, lane_rows)


def _attn_kernel(qp_ref, kp_ref, vt_ref, o_ref, sa_ref, sb_ref):
    qi = pl.program_id(2)
    heads = range(HEADS_PER_STEP)
    qs = [qp_ref[0, hd] for hd in heads]

    def scores(j, s_ref, diag=None):
        k0 = pl.multiple_of(j * BK, BK)
        q0 = 0 if diag is None else diag * BK
        for hd in heads:
            s_ref[hd, :, q0:] = lax.dot_general(kp_ref[0, hd, pl.ds(k0, BK), :], qs[hd][q0:],
                                                (((1,), (1,)), ((), ())),
                                                preferred_element_type=jnp.float32)

    def consume(j, s_ref, carries, diag):
        k0 = pl.multiple_of(j * BK, BK)
        q0 = 0 if diag is None else diag * BK
        out = []
        for hd in heads:
            m, acc = (c[:, q0:] for c in carries[hd])
            s = s_ref[hd, :, q0:]
            if diag is not None:
                kpos = lax.broadcasted_iota(jnp.int32, s.shape, 0) + diag * BK
                qpos = lax.broadcasted_iota(jnp.int32, s.shape, 1) + q0
                s = jnp.where(kpos <= qpos, s, -jnp.inf)
            m_new = jnp.maximum(m, jnp.max(s, axis=0, keepdims=True))
            p = jnp.exp2(s - m_new)
            vblk = vt_ref[0, hd, :, pl.ds(k0, BK)]
            acc = jnp.exp2(m - m_new) * acc + jnp.dot(vblk, p.astype(jnp.bfloat16),
                                                      preferred_element_type=jnp.float32)
            new = (m_new, acc)
            if q0:
                new = tuple(jnp.concatenate([c[:, :q0], n], axis=1) for c, n in zip(carries[hd], new))
            out.append(new)
        return tuple(out)

    def finish(carries):
        for hd, (m, acc) in enumerate(carries):
            o_ref[0, hd * HEAD_DIM:(hd + 1) * HEAD_DIM, :] = (acc[0:HEAD_DIM] / acc[HEAD_DIM:HEAD_DIM + 1]
                                                             ).astype(o_ref.dtype)

    init = ((jnp.full((1, BQ), -jnp.inf, jnp.float32), jnp.zeros((VT_ROWS, BQ), jnp.float32)),) * HEADS_PER_STEP

    @pl.when(qi == 0)
    def _():
        scores(0, sa_ref, 0)
        scores(1, sb_ref, 1)
        finish(consume(1, sb_ref, consume(0, sa_ref, init, 0), 1))

    @pl.when(qi > 0)
    def _():
        def pair(t, carries):
            j = 2 * t + 1
            scores(j + 1, sb_ref)
            carries = consume(j, sa_ref, carries, None)
            scores(j + 2, sa_ref)
            return consume(j + 1, sb_ref, carries, None)

        def pairs(t0, n, carries):
            for u in range(n):
                carries = pair(t0 + u, carries)
            return carries

        scores(1, sa_ref)
        n_pairs = qi - 1
        trips = n_pairs // PAIRS_PER_TRIP
        carries = lax.fori_loop(0, trips, lambda t, cr: pairs(PAIRS_PER_TRIP * t, PAIRS_PER_TRIP, cr), init)
        done = trips * PAIRS_PER_TRIP
        group = PAIRS_PER_TRIP // 2
        while group:
            take = (n_pairs - done) & group
            carries = lax.cond(take != 0, functools.partial(pairs, done, group), lambda cr: cr, carries)
            done, group = done + take, group // 2
        last = 2 * qi - 1
        scores(last + 1, sb_ref, 0)
        carries = consume(last, sa_ref, carries, None)
        scores(last + 2, sa_ref, 1)
        carries = consume(last + 1, sb_ref, carries, 0)
        finish(consume(last + 2, sa_ref, carries, 1))


def _attn_call(qp, kp, vt):
    bsz, heads, l_pad, _ = qp.shape
    nq = l_pad // BQ
    hps = HEADS_PER_STEP
    return pl.pallas_call(
        _attn_kernel,
        grid=(bsz, heads // hps, nq),
        in_specs=[
            pl.BlockSpec((1, hps, BQ, LANES), lambda b, h, i: (b, h, i, 0)),
            pl.BlockSpec((1, hps, l_pad, LANES), lambda b, h, i: (b, h, 0, 0)),
            pl.BlockSpec((1, hps, VT_ROWS, l_pad), lambda b, h, i: (b, h, 0, 0)),
        ],
        out_specs=pl.BlockSpec((1, hps * HEAD_DIM, BQ), lambda b, h, i: (b, h, i)),
        out_shape=jax.ShapeDtypeStruct((bsz, heads * HEAD_DIM, l_pad), jnp.bfloat16),
        scratch_shapes=[pltpu.VMEM((hps, BK, BQ), jnp.float32), pltpu.VMEM((hps, BK, BQ), jnp.float32)],
        compiler_params=pltpu.CompilerParams(
            dimension_semantics=("arbitrary", "arbitrary", "arbitrary"), vmem_limit_bytes=VMEM_LIMIT),
        name="attn",
    )(qp, kp, vt)


def _causal_taps(ext_ref, w_ref, n_taps, k_lo, k_hi, shift_ref=None):
    out = None
    for k in range(k_lo, k_hi):
        start = HALO - (n_taps - 1) + k
        if shift_ref is None or start % SUBLANES == 0:
            window = ext_ref[pl.ds(start, ROW_TILE), :]
        else:
            window = shift_ref[start % SUBLANES - 1, pl.ds(start - start % SUBLANES, ROW_TILE), :]
        term = window * w_ref[k:k + 1, :]
        out = term if out is None else out + term
    return out


def _zero_of(x):
    bits = pltpu.bitcast(x, jnp.uint32)
    return lax.shift_right_logical(lax.shift_right_logical(bits, jnp.uint32(16)), jnp.uint32(16)).astype(jnp.float32)


def _conv_parts(mixt_ref, halo, wdw_ref, bdw_ref, lng_ref, lnb_ref, wsc_ref, ext_a, ext_c, shift_a, y_ref, sc_ref):
    tap_parts = CONV_PARTS - 2
    bounds = [round(i * CONF_K / tap_parts) for i in range(tap_parts + 1)]
    state = {}

    def first_part(edge):
        ext_a[0:HALO, :] = halo[:, 0:D_CONF] + edge
        ext_a[HALO:, :] = mixt_ref[0, :, 0:D_CONF]
        ext_c[0:HALO, :] = halo[:, D_CONF + D_SC:] + edge
        ext_c[HALO:, :] = mixt_ref[0, :, D_CONF + D_SC:]
        for r in range(1, SUBLANES):
            shift_a[r - 1] = ext_a[pl.ds(r, SHIFT_ROWS), :]
        state["dw"] = bdw_ref[...]

    def middle_part(part, edge):
        state["dw"] = state["dw"] + edge + _causal_taps(ext_a, wdw_ref, CONF_K, bounds[part - 1], bounds[part],
                                                        shift_a)

    def last_part(edge):
        dw = state["dw"] + edge
        mu = jnp.mean(dw, axis=-1, keepdims=True)
        xc = dw - mu
        y = xc * lax.rsqrt(jnp.mean(xc * xc, axis=-1, keepdims=True) + EPS) * lng_ref[...] + lnb_ref[...]
        y = y * jax.nn.sigmoid(y)
        sc = mixt_ref[0, :, D_CONF:D_CONF + D_SC] * _causal_taps(ext_c, wsc_ref, SC_K, 0, SC_K)
        y_ref[...] = y.astype(y_ref.dtype)
        sc_ref[...] = sc.astype(sc_ref.dtype)
        return _zero_of(jnp.max(y, axis=0, keepdims=True) + jnp.max(sc, axis=0, keepdims=True))

    middle = [functools.partial(middle_part, part) for part in range(1, CONV_PARTS - 1)]
    return [first_part] + middle + [last_part]


def _mix_kernel(src_ref, meta_ref, ot_ref, mix_ref, mixn_ref, wdw_ref, bdw_ref, lng_ref, lnb_ref, wpw_ref,
                bpw_ref, wsc_ref, wout_ref, ng_ref, w1_ref, w2_ref, fg_ref, out_ref,
                ext_a, ext_c, shift_a, y_ref, sc_ref, *, first, final):
    conv_parts = functools.partial(_conv_parts, wdw_ref=wdw_ref, bdw_ref=bdw_ref, lng_ref=lng_ref, lnb_ref=lnb_ref,
                                   wsc_ref=wsc_ref, ext_a=ext_a, ext_c=ext_c, shift_a=shift_a,
                                   y_ref=y_ref, sc_ref=sc_ref)

    @pl.when(pl.program_id(1) == 0)
    def _():
        for part in conv_parts(mix_ref, jnp.zeros((HALO, D_CONF + 2 * D_SC), jnp.float32)):
            part(0.0)

    conf = jnp.dot(y_ref[...], wpw_ref[...], preferred_element_type=jnp.float32) + bpw_ref[...]
    sc = sc_ref[...]

    attn = ot_ref[0].T
    mixed = (jnp.dot(attn, wout_ref[0:D_ATTN, :], preferred_element_type=jnp.float32)
             + jnp.dot(conf.astype(jnp.bfloat16), wout_ref[D_ATTN:D_ATTN + D_CONF, :],
                       preferred_element_type=jnp.float32)
             + jnp.dot(sc, wout_ref[D_ATTN + D_CONF:, :], preferred_element_type=jnp.float32))
    h1 = _tile_rows(src_ref, meta_ref, first) + mixed

    next_parts = conv_parts(mixn_ref, mix_ref[0, ROW_TILE - HALO:, :])
    hn = _rms(h1, ng_ref[...]).astype(jnp.bfloat16)
    acc = h1
    for chunk, c0 in enumerate(range(0, D_FF, FF_CHUNK)):
        if chunk == CONV_PARTS:
            hn = hn + jnp.concatenate([conv_zero] * (D_MODEL // D_CONF), axis=1).astype(jnp.bfloat16)
        a = jnp.dot(hn, w1_ref[:, c0:c0 + FF_CHUNK], preferred_element_type=jnp.float32)
        if chunk < CONV_PARTS:
            conv_zero = next_parts[chunk](_zero_of(a[0:1, 0:D_CONF]))
        a = jnp.maximum(a, 0.0)
        acc = acc + jnp.dot((a * a).astype(jnp.bfloat16), w2_ref[c0:c0 + FF_CHUNK, :],
                            preferred_element_type=jnp.float32)
    if final:
        out_ref[0] = _rms(acc, fg_ref[...])
    else:
        out_ref[0] = jnp.where(_pad_rows_below(acc.shape), 0.0, acc)


def _mix_call(src, meta_tile, ot, mixin, wdw, bdw, lng, lnb, wpw, bpw, wsc, wout, ng, w1, w2, fg,
              *, layer, first, final, l_pad):
    bsz = src.shape[0]
    nt = l_pad // ROW_TILE
    const = lambda b, i: (0, 0)
    once = pl.Buffered(1)
    wide = D_CONF + 2 * D_SC
    if final:
        out_spec = pl.BlockSpec((1, ROW_TILE, D_MODEL), lambda b, i: (b, jnp.maximum(i - 1, 0), 0))
        out_shape = jax.ShapeDtypeStruct((bsz, l_pad - ROW_TILE, D_MODEL), jnp.float32)
    else:
        out_spec = pl.BlockSpec((1, ROW_TILE, D_MODEL), lambda b, i: (b, i, 0))
        out_shape = jax.ShapeDtypeStruct((bsz, l_pad, D_MODEL), jnp.float32)
    return pl.pallas_call(
        functools.partial(_mix_kernel, first=first, final=final),
        grid=(bsz, nt),
        in_specs=[
            _src_spec(first),
            pl.BlockSpec((ROW_TILE, D_MODEL), const),
            pl.BlockSpec((1, D_ATTN, ROW_TILE), lambda b, i: (b, 0, i)),
            pl.BlockSpec((1, ROW_TILE, wide), lambda b, i: (b, i, 0)),
            pl.BlockSpec((1, ROW_TILE, wide), lambda b, i: (b, jnp.minimum(i + 1, nt - 1), 0)),
            _layer_spec((HALO, D_CONF), layer),
            _layer_spec((1, D_CONF), layer),
            _layer_spec((1, D_CONF), layer),
            _layer_spec((1, D_CONF), layer),
            _layer_spec((D_CONF, D_CONF), layer),
            _layer_spec((1, D_CONF), layer),
            _layer_spec((SUBLANES, D_SC), layer),
            _layer_spec((D_MODEL, D_MODEL), layer, pipeline_mode=once),
            _layer_spec((1, D_MODEL), layer),
            _layer_spec((D_MODEL, D_FF), layer, pipeline_mode=once),
            _layer_spec((D_FF, D_MODEL), layer, pipeline_mode=once),
            pl.BlockSpec((1, D_MODEL), const),
        ],
        out_specs=out_spec,
        out_shape=out_shape,
        scratch_shapes=[pltpu.VMEM((HALO + ROW_TILE, D_CONF), jnp.float32),
                        pltpu.VMEM((HALO + ROW_TILE, D_SC), jnp.float32),
                        pltpu.VMEM((SUBLANES - 1, SHIFT_ROWS, D_CONF), jnp.float32),
                        pltpu.VMEM((ROW_TILE, D_CONF), jnp.bfloat16),
                        pltpu.VMEM((ROW_TILE, D_SC), jnp.bfloat16)],
        compiler_params=pltpu.CompilerParams(
            dimension_semantics=("arbitrary", "arbitrary"), vmem_limit_bytes=VMEM_LIMIT),
        name="mix_final" if final else "mix",
    )(src, meta_tile, ot, mixin, mixin, wdw, bdw, lng, lnb, wpw, bpw, wsc, wout, ng, w1, w2, fg)


def _decay_selectors():
    assert HEAD_DIM + HEADS * DECAY_LANES <= LANES
    e = np.zeros((N_SPLIT * LANES, 2 * LANES), np.float32)
    bias = np.zeros((1, 2 * LANES), np.float32)
    lane_rows = np.zeros((2 * HEADS, LANES), np.float32)
    for hd in range(HEADS):
        lane0 = HEAD_DIM + hd * DECAY_LANES
        lane_rows[hd, lane0:lane0 + DECAY_LANES] = 1.0
        lane_rows[HEADS, lane0 + N_SPLIT] = PAD_KEY_BIAS
        for j in range(N_SPLIT):
            e[j * LANES + hd, lane0 + j] = 1.0
            bias[0, lane0 + N_SPLIT + j] = 1.0
            bias[0, LANES + lane0 + j] = 1.0
            e[j * LANES + hd, LANES + lane0 + N_SPLIT + j] = -1.0
    return jnp.asarray(e, jnp.bfloat16), jnp.asarray(bias, jnp.float32), jnp.asarray(lane_rows, jnp.float32)


def _pack_w_in(w):
    q = w[..., 0:D_ATTN] * (HEAD_DIM ** -0.5)
    k = w[..., D_ATTN:2 * D_ATTN]
    v = w[..., 2 * D_ATTN:3 * D_ATTN]
    f = jnp.pad(w[..., 3 * D_ATTN:3 * D_ATTN + HEADS], ((0, 0), (0, 0), (0, LANES - HEADS)))
    rest = w[..., 3 * D_ATTN + HEADS:]
    return jnp.concatenate([f, q, k, v, rest], -1).astype(jnp.bfloat16)


def kernel(x, meta_tokens, mix_norm_g, w_in, b_forget, w_conf_dw, b_conf_dw, conf_ln_g, conf_ln_b,
           w_conf_pw, b_conf_pw, w_sc_conv, w_out, mlp_norm_g, w_mlp1, w_mlp2, final_norm_g):
    bsz, seq, d = x.shape
    depth = w_in.shape[0]
    assert seq % ROW_TILE == 0 and meta_tokens.shape[0] == N_META and PAD_FRONT >= BK and BQ == 2 * BK
    l_pad = ROW_TILE + seq

    meta_tile = jnp.concatenate([jnp.zeros((PAD_FRONT, d), x.dtype), meta_tokens.astype(x.dtype)], axis=0)
    tri = jnp.asarray(np.tril(np.ones((ROW_TILE, ROW_TILE), np.float32)), jnp.bfloat16)
    e_mat, qk_bias, lane_rows = _decay_selectors()

    rows = lambda a: a[:, None, :].astype(jnp.float32)
    bf16 = lambda a: a.astype(jnp.bfloat16)
    w_pack, bf_rows = _pack_w_in(w_in), jnp.pad(rows(b_forget), ((0, 0), (0, 0), (0, LANES - HEADS)))
    wdw = jnp.pad(w_conf_dw, ((0, 0), (0, HALO - CONF_K), (0, 0)))
    wsc = jnp.pad(w_sc_conv, ((0, 0), (0, SUBLANES - SC_K), (0, 0)))
    mix_params = (wdw, rows(b_conf_dw), rows(conf_ln_g), rows(conf_ln_b), bf16(w_conf_pw), rows(b_conf_pw), wsc,
                  bf16(w_out), rows(mlp_norm_g), bf16(w_mlp1), bf16(w_mlp2),
                  final_norm_g.reshape(1, -1).astype(jnp.float32))
    g_rows = rows(mix_norm_g)

    h = x
    for l in range(depth):
        first, final = l == 0, l == depth - 1
        qp, kp, vt, mixin = _proj_call(h, meta_tile, g_rows, w_pack, bf_rows, tri, e_mat, qk_bias, lane_rows,
                                       layer=l, first=first, l_pad=l_pad)
        ot = _attn_call(qp, kp, vt)
        h = _mix_call(h, meta_tile, ot, mixin, *mix_params, layer=l, first=first, final=final, l_pad=l_pad)
    return h
```

```python
import functools

import jax
import jax.numpy as jnp
import numpy as np
from jax import lax
from jax.experimental import pallas as pl
from jax.experimental.pallas import tpu as pltpu

D_MODEL = 1024
N_META = 16
HEADS = 8
HEAD_DIM = 64
D_ATTN = HEADS * HEAD_DIM
D_CONF = 256
D_SC = 256
CONF_K = 31
SC_K = 3
D_FF = 4 * D_MODEL
EPS = 1e-6

LANES = 128
SUBLANES = 8
ROW_TILE = 512
PAD_FRONT = ROW_TILE - N_META
BQ = ROW_TILE
BK = 256
HEADS_PER_STEP = 2
PAIRS_PER_TRIP = 8
LOG2E = 1.4426950408889634
VT_ROWS = HEAD_DIM + 16
PAD_KEY_BIAS = -2.0 ** 100
HALO = 32
SHIFT_ROWS = ROW_TILE + HALO - SUBLANES
FF_CHUNK = 512
CONV_PARTS = 7
N_SPLIT = 3
VMEM_LIMIT = 56 * 1024 * 1024

C_F = 0
C_Q = C_F + LANES
C_K = C_Q + D_ATTN
C_V = C_K + D_ATTN
C_REST = C_V + D_ATTN
N_PACK = C_REST + 2 * D_CONF + 3 * D_SC
DECAY_LANES = 2 * N_SPLIT


def _split_bf16(x):
    pieces = []
    r = x
    for _ in range(N_SPLIT):
        p = r.astype(jnp.bfloat16)
        pieces.append(p)
        r = r - p.astype(jnp.float32)
    return pieces


def _rms(x, g):
    return x * lax.rsqrt(jnp.mean(x * x, axis=-1, keepdims=True) + EPS) * g


def _tile_rows(src_ref, meta_ref, first):
    h = src_ref[0]
    if first:
        h = jnp.where(pl.program_id(1) == 0, meta_ref[...], h)
    return h


def _pad_rows_below(shape):
    limit = jnp.where(pl.program_id(1) == 0, PAD_FRONT, 0)
    return lax.broadcasted_iota(jnp.int32, shape, 0) < limit


def _proj_kernel(src_ref, meta_ref, g_ref, w_ref, bf_ref, tri_ref, e_ref, qkb_ref, lanes_ref,
                 qp_ref, kp_ref, vt_ref, mix_ref, carry_ref, *, first):
    @pl.when(pl.program_id(1) == 0)
    def _():
        carry_ref[...] = jnp.zeros_like(carry_ref)

    hn = _rms(_tile_rows(src_ref, meta_ref, first), g_ref[...]).astype(jnp.bfloat16)

    fq = jnp.dot(hn, w_ref[:, C_F:C_K], preferred_element_type=jnp.float32)
    z = fq[:, :LANES] + bf_ref[...]
    log_f = jnp.minimum(z, 0.0) - jnp.log1p(jnp.exp(-jnp.abs(z)))
    sums = jnp.dot(tri_ref[...], jnp.concatenate(_split_bf16(log_f), axis=-1),
                   preferred_element_type=jnp.float32)
    c = carry_ref[0:1, :]
    for piece in range(N_SPLIT):
        c = c + sums[:, piece * LANES:(piece + 1) * LANES]
    carry_ref[0:1, :] = c[ROW_TILE - 1:ROW_TILE, :]

    c_pieces = jnp.concatenate(_split_bf16(c * LOG2E), axis=-1)
    decay = jnp.dot(c_pieces, e_ref[...], preferred_element_type=jnp.float32) + qkb_ref[...]
    decay_q, decay_k = decay[:, :LANES], decay[:, LANES:]
    decay_k = decay_k + jnp.where(_pad_rows_below((ROW_TILE, LANES)), lanes_ref[HEADS:HEADS + 1, :], 0.0)
    q = fq[:, LANES:] * LOG2E
    k = jnp.dot(hn, w_ref[:, C_K:C_V], preferred_element_type=jnp.float32)
    low_half = lax.broadcasted_iota(jnp.int32, (ROW_TILE, LANES), 1) < HEAD_DIM
    for hd in range(HEADS):
        pair_cols = slice((hd // 2) * LANES, (hd // 2 + 1) * LANES)
        q_h, k_h = q[:, pair_cols], k[:, pair_cols]
        if hd % 2:
            q_h, k_h = pltpu.roll(q_h, HEAD_DIM, axis=1), pltpu.roll(k_h, HEAD_DIM, axis=1)
        qp_ref[0, hd] = jnp.where(low_half, q_h, decay_q).astype(jnp.bfloat16)
        kp_ref[0, hd] = jnp.where(low_half, k_h, decay_k * lanes_ref[hd:hd + 1, :]).astype(jnp.bfloat16)

    v = jnp.dot(hn, w_ref[:, C_V:C_REST], preferred_element_type=jnp.float32)
    vt = v.T.astype(jnp.bfloat16)
    ones_row = (lax.broadcasted_iota(jnp.int32, (VT_ROWS - HEAD_DIM, ROW_TILE), 0) == 0).astype(jnp.bfloat16)
    for hd in range(HEADS):
        vt_ref[0, hd, 0:HEAD_DIM, :] = vt[hd * HEAD_DIM:(hd + 1) * HEAD_DIM, :]
        vt_ref[0, hd, HEAD_DIM:, :] = ones_row

    r = jnp.dot(hn, w_ref[:, C_REST:N_PACK], preferred_element_type=jnp.float32)
    conf_a = r[:, 0:D_CONF]
    conf_g = r[:, D_CONF:2 * D_CONF]
    sc_b = r[:, 2 * D_CONF:2 * D_CONF + D_SC]
    sc_c = r[:, 2 * D_CONF + D_SC:2 * D_CONF + 2 * D_SC]
    sc_u = r[:, 2 * D_CONF + 2 * D_SC:]
    mix_ref[0, :, 0:D_CONF] = conf_a * jax.nn.sigmoid(conf_g)
    mix_ref[0, :, D_CONF:D_CONF + D_SC] = sc_b
    mix_ref[0, :, D_CONF + D_SC:] = sc_c * sc_u


def _src_spec(first):
    if first:
        return pl.BlockSpec((1, ROW_TILE, D_MODEL), lambda b, i: (b, jnp.maximum(i - 1, 0), 0))
    return pl.BlockSpec((1, ROW_TILE, D_MODEL), lambda b, i: (b, i, 0))


def _layer_spec(shape, layer, **kwargs):
    return pl.BlockSpec((None,) + shape, lambda b, i: (layer,) + (0,) * len(shape), **kwargs)


def _proj_call(src, meta_tile, g, w_pack, bf_row, tri, e_mat, qk_bias, lane_rows, *, layer, first, l_pad):
    bsz = src.shape[0]
    nt = l_pad // ROW_TILE
    const = lambda b, i: (0, 0)
    return pl.pallas_call(
        functools.partial(_proj_kernel, first=first),
        grid=(bsz, nt),
        in_specs=[
            _src_spec(first),
            pl.BlockSpec((ROW_TILE, D_MODEL), const),
            _layer_spec((1, D_MODEL), layer),
            _layer_spec((D_MODEL, N_PACK), layer),
            _layer_spec((1, LANES), layer),
            pl.BlockSpec((ROW_TILE, ROW_TILE), const),
            pl.BlockSpec((N_SPLIT * LANES, 2 * LANES), const),
            pl.BlockSpec((1, 2 * LANES), const),
            pl.BlockSpec((2 * HEADS, LANES), const),
        ],
        out_specs=[
            pl.BlockSpec((1, HEADS, ROW_TILE, LANES), lambda b, i: (b, 0, i, 0)),
            pl.BlockSpec((1, HEADS, ROW_TILE, LANES), lambda b, i: (b, 0, i, 0)),
            pl.BlockSpec((1, HEADS, VT_ROWS, ROW_TILE), lambda b, i: (b, 0, 0, i)),
            pl.BlockSpec((1, ROW_TILE, D_CONF + 2 * D_SC), lambda b, i: (b, i, 0)),
        ],
        out_shape=[
            jax.ShapeDtypeStruct((bsz, HEADS, l_pad, LANES), jnp.bfloat16),
            jax.ShapeDtypeStruct((bsz, HEADS, l_pad, LANES), jnp.bfloat16),
            jax.ShapeDtypeStruct((bsz, HEADS, VT_ROWS, l_pad), jnp.bfloat16),
            jax.ShapeDtypeStruct((bsz, l_pad, D_CONF + 2 * D_SC), jnp.float32),
        ],
        scratch_shapes=[pltpu.VMEM((8, LANES), jnp.float32)],
        compiler_params=pltpu.CompilerParams(
            dimension_semantics=("arbitrary", "arbitrary"), vmem_limit_bytes=VMEM_LIMIT),
        name="proj_first" if first else "proj",
    )(src, meta_tile, g, w_pack, bf_row, tri, e_mat, qk_bias, lane_rows)


def _attn_kernel(qp_ref, kp_ref, vt_ref, o_ref, sa_ref, sb_ref):
    qi = pl.program_id(2)
    heads = range(HEADS_PER_STEP)
    qs = [qp_ref[0, hd] for hd in heads]

    def scores(j, s_ref, diag=None):
        k0 = pl.multiple_of(j * BK, BK)
        q0 = 0 if diag is None else diag * BK
        for hd in heads:
            s_ref[hd, :, q0:] = lax.dot_general(kp_ref[0, hd, pl.ds(k0, BK), :], qs[hd][q0:],
                                                (((1,), (1,)), ((), ())),
                                                preferred_element_type=jnp.float32)

    def consume(j, s_ref, carries, diag):
        k0 = pl.multiple_of(j * BK, BK)
        q0 = 0 if diag is None else diag * BK
        out = []
        for hd in heads:
            m, acc = (c[:, q0:] for c in carries[hd])
            s = s_ref[hd, :, q0:]
            if diag is not None:
                kpos = lax.broadcasted_iota(jnp.int32, s.shape, 0) + diag * BK
                qpos = lax.broadcasted_iota(jnp.int32, s.shape, 1) + q0
                s = jnp.where(kpos <= qpos, s, -jnp.inf)
            m_new = jnp.maximum(m, jnp.max(s, axis=0, keepdims=True))
            p = jnp.exp2(s - m_new)
            vblk = vt_ref[0, hd, :, pl.ds(k0, BK)]
            acc = jnp.exp2(m - m_new) * acc + jnp.dot(vblk, p.astype(jnp.bfloat16),
                                                      preferred_element_type=jnp.float32)
            new = (m_new, acc)
            if q0:
                new = tuple(jnp.concatenate([c[:, :q0], n], axis=1) for c, n in zip(carries[hd], new))
            out.append(new)
        return tuple(out)

    def finish(carries):
        for hd, (m, acc) in enumerate(carries):
            o_ref[0, hd * HEAD_DIM:(hd + 1) * HEAD_DIM, :] = (acc[0:HEAD_DIM] / acc[HEAD_DIM:HEAD_DIM + 1]
                                                             ).astype(o_ref.dtype)

    init = ((jnp.full((1, BQ), -jnp.inf, jnp.float32), jnp.zeros((VT_ROWS, BQ), jnp.float32)),) * HEADS_PER_STEP

    @pl.when(qi == 0)
    def _():
        scores(0, sa_ref, 0)
        scores(1, sb_ref, 1)
        finish(consume(1, sb_ref, consume(0, sa_ref, init, 0), 1))

    @pl.when(qi > 0)
    def _():
        def pair(t, carries):
            j = 2 * t + 1
            scores(j + 1, sb_ref)
            carries = consume(j, sa_ref, carries, None)
            scores(j + 2, sa_ref)
            return consume(j + 1, sb_ref, carries, None)

        def pairs(t0, n, carries):
            for u in range(n):
                carries = pair(t0 + u, carries)
            return carries

        scores(1, sa_ref)
        n_pairs = qi - 1
        trips = n_pairs // PAIRS_PER_TRIP
        carries = lax.fori_loop(0, trips, lambda t, cr: pairs(PAIRS_PER_TRIP * t, PAIRS_PER_TRIP, cr), init)
        done = trips * PAIRS_PER_TRIP
        group = PAIRS_PER_TRIP // 2
        while group:
            take = (n_pairs - done) & group
            carries = lax.cond(take != 0, functools.partial(pairs, done, group), lambda cr: cr, carries)
            done, group = done + take, group // 2
        last = 2 * qi - 1
        scores(last + 1, sb_ref, 0)
        carries = consume(last, sa_ref, carries, None)
        scores(last + 2, sa_ref, 1)
        carries = consume(last + 1, sb_ref, carries, 0)
        finish(consume(last + 2, sa_ref, carries, 1))


def _attn_call(qp, kp, vt):
    bsz, heads, l_pad, _ = qp.shape
    nq = l_pad // BQ
    hps = HEADS_PER_STEP
    return pl.pallas_call(
        _attn_kernel,
        grid=(bsz, heads // hps, nq),
        in_specs=[
            pl.BlockSpec((1, hps, BQ, LANES), lambda b, h, i: (b, h, i, 0)),
            pl.BlockSpec((1, hps, l_pad, LANES), lambda b, h, i: (b, h, 0, 0)),
            pl.BlockSpec((1, hps, VT_ROWS, l_pad), lambda b, h, i: (b, h, 0, 0)),
        ],
        out_specs=pl.BlockSpec((1, hps * HEAD_DIM, BQ), lambda b, h, i: (b, h, i)),
        out_shape=jax.ShapeDtypeStruct((bsz, heads * HEAD_DIM, l_pad), jnp.bfloat16),
        scratch_shapes=[pltpu.VMEM((hps, BK, BQ), jnp.float32), pltpu.VMEM((hps, BK, BQ), jnp.float32)],
        compiler_params=pltpu.CompilerParams(
            dimension_semantics=("arbitrary", "arbitrary", "arbitrary"), vmem_limit_bytes=VMEM_LIMIT),
        name="attn",
    )(qp, kp, vt)


def _causal_taps(ext_ref, w_ref, n_taps, k_lo, k_hi, shift_ref=None):
    out = None
    for k in range(k_lo, k_hi):
        start = HALO - (n_taps - 1) + k
        if shift_ref is None or start % SUBLANES == 0:
            window = ext_ref[pl.ds(start, ROW_TILE), :]
        else:
            window = shift_ref[start % SUBLANES - 1, pl.ds(start - start % SUBLANES, ROW_TILE), :]
        term = window * w_ref[k:k + 1, :]
        out = term if out is None else out + term
    return out


def _zero_of(x):
    bits = pltpu.bitcast(x, jnp.uint32)
    return lax.shift_right_logical(lax.shift_right_logical(bits, jnp.uint32(16)), jnp.uint32(16)).astype(jnp.float32)


def _conv_parts(mixt_ref, halo, wdw_ref, bdw_ref, lng_ref, lnb_ref, wsc_ref, ext_a, ext_c, shift_a, y_ref, sc_ref):
    tap_parts = CONV_PARTS - 2
    bounds = [round(i * CONF_K / tap_parts) for i in range(tap_parts + 1)]
    state = {}

    def first_part(edge):
        ext_a[0:HALO, :] = halo[:, 0:D_CONF] + edge
        ext_a[HALO:, :] = mixt_ref[0, :, 0:D_CONF]
        ext_c[0:HALO, :] = halo[:, D_CONF + D_SC:] + edge
        ext_c[HALO:, :] = mixt_ref[0, :, D_CONF + D_SC:]
        for r in range(1, SUBLANES):
            shift_a[r - 1] = ext_a[pl.ds(r, SHIFT_ROWS), :]
        state["dw"] = bdw_ref[...]

    def middle_part(part, edge):
        state["dw"] = state["dw"] + edge + _causal_taps(ext_a, wdw_ref, CONF_K, bounds[part - 1], bounds[part],
                                                        shift_a)

    def last_part(edge):
        dw = state["dw"] + edge
        mu = jnp.mean(dw, axis=-1, keepdims=True)
        xc = dw - mu
        y = xc * lax.rsqrt(jnp.mean(xc * xc, axis=-1, keepdims=True) + EPS) * lng_ref[...] + lnb_ref[...]
        y = y * jax.nn.sigmoid(y)
        sc = mixt_ref[0, :, D_CONF:D_CONF + D_SC] * _causal_taps(ext_c, wsc_ref, SC_K, 0, SC_K)
        y_ref[...] = y.astype(y_ref.dtype)
        sc_ref[...] = sc.astype(sc_ref.dtype)
        return _zero_of(jnp.max(y, axis=0, keepdims=True) + jnp.max(sc, axis=0, keepdims=True))

    middle = [functools.partial(middle_part, part) for part in range(1, CONV_PARTS - 1)]
    return [first_part] + middle + [last_part]


def _mix_kernel(src_ref, meta_ref, ot_ref, mix_ref, mixn_ref, wdw_ref, bdw_ref, lng_ref, lnb_ref, wpw_ref,
                bpw_ref, wsc_ref, wout_ref, ng_ref, w1_ref, w2_ref, fg_ref, out_ref,
                ext_a, ext_c, shift_a, y_ref, sc_ref, *, first, final):
    conv_parts = functools.partial(_conv_parts, wdw_ref=wdw_ref, bdw_ref=bdw_ref, lng_ref=lng_ref, lnb_ref=lnb_ref,
                                   wsc_ref=wsc_ref, ext_a=ext_a, ext_c=ext_c, shift_a=shift_a,
                                   y_ref=y_ref, sc_ref=sc_ref)

    @pl.when(pl.program_id(1) == 0)
    def _():
        for part in conv_parts(mix_ref, jnp.zeros((HALO, D_CONF + 2 * D_SC), jnp.float32)):
            part(0.0)

    conf = jnp.dot(y_ref[...], wpw_ref[...], preferred_element_type=jnp.float32) + bpw_ref[...]
    sc = sc_ref[...]

    attn = ot_ref[0].T
    mixed = (jnp.dot(attn, wout_ref[0:D_ATTN, :], preferred_element_type=jnp.float32)
             + jnp.dot(conf.astype(jnp.bfloat16), wout_ref[D_ATTN:D_ATTN + D_CONF, :],
                       preferred_element_type=jnp.float32)
             + jnp.dot(sc, wout_ref[D_ATTN + D_CONF:, :], preferred_element_type=jnp.float32))
    h1 = _tile_rows(src_ref, meta_ref, first) + mixed

    next_parts = conv_parts(mixn_ref, mix_ref[0, ROW_TILE - HALO:, :])
    hn = _rms(h1, ng_ref[...]).astype(jnp.bfloat16)
    acc = h1
    for chunk, c0 in enumerate(range(0, D_FF, FF_CHUNK)):
        if chunk == CONV_PARTS:
            hn = hn + jnp.concatenate([conv_zero] * (D_MODEL // D_CONF), axis=1).astype(jnp.bfloat16)
        a = jnp.dot(hn, w1_ref[:, c0:c0 + FF_CHUNK], preferred_element_type=jnp.float32)
        if chunk < CONV_PARTS:
            conv_zero = next_parts[chunk](_zero_of(a[0:1, 0:D_CONF]))
        a = jnp.maximum(a, 0.0)
        acc = acc + jnp.dot((a * a).astype(jnp.bfloat16), w2_ref[c0:c0 + FF_CHUNK, :],
                            preferred_element_type=jnp.float32)
    if final:
        out_ref[0] = _rms(acc, fg_ref[...])
    else:
        out_ref[0] = jnp.where(_pad_rows_below(acc.shape), 0.0, acc)


def _mix_call(src, meta_tile, ot, mixin, wdw, bdw, lng, lnb, wpw, bpw, wsc, wout, ng, w1, w2, fg,
              *, layer, first, final, l_pad):
    bsz = src.shape[0]
    nt = l_pad // ROW_TILE
    const = lambda b, i: (0, 0)
    once = pl.Buffered(1)
    wide = D_CONF + 2 * D_SC
    if final:
        out_spec = pl.BlockSpec((1, ROW_TILE, D_MODEL), lambda b, i: (b, jnp.maximum(i - 1, 0), 0))
        out_shape = jax.ShapeDtypeStruct((bsz, l_pad - ROW_TILE, D_MODEL), jnp.float32)
    else:
        out_spec = pl.BlockSpec((1, ROW_TILE, D_MODEL), lambda b, i: (b, i, 0))
        out_shape = jax.ShapeDtypeStruct((bsz, l_pad, D_MODEL), jnp.float32)
    return pl.pallas_call(
        functools.partial(_mix_kernel, first=first, final=final),
        grid=(bsz, nt),
        in_specs=[
            _src_spec(first),
            pl.BlockSpec((ROW_TILE, D_MODEL), const),
            pl.BlockSpec((1, D_ATTN, ROW_TILE), lambda b, i: (b, 0, i)),
            pl.BlockSpec((1, ROW_TILE, wide), lambda b, i: (b, i, 0)),
            pl.BlockSpec((1, ROW_TILE, wide), lambda b, i: (b, jnp.minimum(i + 1, nt - 1), 0)),
            _layer_spec((HALO, D_CONF), layer),
            _layer_spec((1, D_CONF), layer),
            _layer_spec((1, D_CONF), layer),
            _layer_spec((1, D_CONF), layer),
            _layer_spec((D_CONF, D_CONF), layer),
            _layer_spec((1, D_CONF), layer),
            _layer_spec((SUBLANES, D_SC), layer),
            _layer_spec((D_MODEL, D_MODEL), layer, pipeline_mode=once),
            _layer_spec((1, D_MODEL), layer),
            _layer_spec((D_MODEL, D_FF), layer, pipeline_mode=once),
            _layer_spec((D_FF, D_MODEL), layer, pipeline_mode=once),
            pl.BlockSpec((1, D_MODEL), const),
        ],
        out_specs=out_spec,
        out_shape=out_shape,
        scratch_shapes=[pltpu.VMEM((HALO + ROW_TILE, D_CONF), jnp.float32),
                        pltpu.VMEM((HALO + ROW_TILE, D_SC), jnp.float32),
                        pltpu.VMEM((SUBLANES - 1, SHIFT_ROWS, D_CONF), jnp.float32),
                        pltpu.VMEM((ROW_TILE, D_CONF), jnp.bfloat16),
                        pltpu.VMEM((ROW_TILE, D_SC), jnp.bfloat16)],
        compiler_params=pltpu.CompilerParams(
            dimension_semantics=("arbitrary", "arbitrary"), vmem_limit_bytes=VMEM_LIMIT),
        name="mix_final" if final else "mix",
    )(src, meta_tile, ot, mixin, mixin, wdw, bdw, lng, lnb, wpw, bpw, wsc, wout, ng, w1, w2, fg)


def _decay_selectors():
    assert HEAD_DIM + HEADS * DECAY_LANES <= LANES
    e = np.zeros((N_SPLIT * LANES, 2 * LANES), np.float32)
    bias = np.zeros((1, 2 * LANES), np.float32)
    lane_rows = np.zeros((2 * HEADS, LANES), np.float32)
    for hd in range(HEADS):
        lane0 = HEAD_DIM + hd * DECAY_LANES
        lane_rows[hd, lane0:lane0 + DECAY_LANES] = 1.0
        lane_rows[HEADS, lane0 + N_SPLIT] = PAD_KEY_BIAS
        for j in range(N_SPLIT):
            e[j * LANES + hd, lane0 + j] = 1.0
            bias[0, lane0 + N_SPLIT + j] = 1.0
            bias[0, LANES + lane0 + j] = 1.0
            e[j * LANES + hd, LANES + lane0 + N_SPLIT + j] = -1.0
    return jnp.asarray(e, jnp.bfloat16), jnp.asarray(bias, jnp.float32), jnp.asarray(lane_rows, jnp.float32)


def _pack_w_in(w):
    q = w[..., 0:D_ATTN] * (HEAD_DIM ** -0.5)
    k = w[..., D_ATTN:2 * D_ATTN]
    v = w[..., 2 * D_ATTN:3 * D_ATTN]
    f = jnp.pad(w[..., 3 * D_ATTN:3 * D_ATTN + HEADS], ((0, 0), (0, 0), (0, LANES - HEADS)))
    rest = w[..., 3 * D_ATTN + HEADS:]
    return jnp.concatenate([f, q, k, v, rest], -1).astype(jnp.bfloat16)


def kernel(x, meta_tokens, mix_norm_g, w_in, b_forget, w_conf_dw, b_conf_dw, conf_ln_g, conf_ln_b,
           w_conf_pw, b_conf_pw, w_sc_conv, w_out, mlp_norm_g, w_mlp1, w_mlp2, final_norm_g):
    bsz, seq, d = x.shape
    depth = w_in.shape[0]
    assert seq % ROW_TILE == 0 and meta_tokens.shape[0] == N_META and PAD_FRONT >= BK and BQ == 2 * BK
    l_pad = ROW_TILE + seq

    meta_tile = jnp.concatenate([jnp.zeros((PAD_FRONT, d), x.dtype), meta_tokens.astype(x.dtype)], axis=0)
    tri = jnp.asarray(np.tril(np.ones((ROW_TILE, ROW_TILE), np.float32)), jnp.bfloat16)
    e_mat, qk_bias, lane_rows = _decay_selectors()

    rows = lambda a: a[:, None, :].astype(jnp.float32)
    bf16 = lambda a: a.astype(jnp.bfloat16)
    w_pack, bf_rows = _pack_w_in(w_in), jnp.pad(rows(b_forget), ((0, 0), (0, 0), (0, LANES - HEADS)))
    wdw = jnp.pad(w_conf_dw, ((0, 0), (0, HALO - CONF_K), (0, 0)))
    wsc = jnp.pad(w_sc_conv, ((0, 0), (0, SUBLANES - SC_K), (0, 0)))
    mix_params = (wdw, rows(b_conf_dw), rows(conf_ln_g), rows(conf_ln_b), bf16(w_conf_pw), rows(b_conf_pw), wsc,
                  bf16(w_out), rows(mlp_norm_g), bf16(w_mlp1), bf16(w_mlp2),
                  final_norm_g.reshape(1, -1).astype(jnp.float32))
    g_rows = rows(mix_norm_g)

    h = x
    for l in range(depth):
        first, final = l == 0, l == depth - 1
        qp, kp, vt, mixin = _proj_call(h, meta_tile, g_rows, w_pack, bf_rows, tri, e_mat, qk_bias, lane_rows,
                                       layer=l, first=first, l_pad=l_pad)
        ot = _attn_call(qp, kp, vt)
        h = _mix_call(h, meta_tile, ot, mixin, *mix_params, layer=l, first=first, final=final, l_pad=l_pad)
    return h
```

```python
import functools

import jax
import jax.numpy as jnp
import numpy as np
from jax import lax
from jax.experimental import pallas as pl
from jax.experimental.pallas import tpu as pltpu

D_MODEL = 1024
N_META = 16
HEADS = 8
HEAD_DIM = 64
D_ATTN = HEADS * HEAD_DIM
D_CONF = 256
D_SC = 256
CONF_K = 31
SC_K = 3
D_FF = 4 * D_MODEL
EPS = 1e-6

LANES = 128
SUBLANES = 8
ROW_TILE = 512
PAD_FRONT = ROW_TILE - N_META
BQ = ROW_TILE
BK = 256
HEADS_PER_STEP = 2
PAIRS_PER_TRIP = 8
LOG2E = 1.4426950408889634
VT_ROWS = HEAD_DIM + 16
PAD_KEY_BIAS = -2.0 ** 100
HALO = 32
SHIFT_ROWS = ROW_TILE + HALO - SUBLANES
FF_CHUNK = 512
CONV_PARTS = 7
N_SPLIT = 3
VMEM_LIMIT = 56 * 1024 * 1024

C_F = 0
C_Q = C_F + LANES
C_K = C_Q + D_ATTN
C_V = C_K + D_ATTN
C_REST = C_V + D_ATTN
N_PACK = C_REST + 2 * D_CONF + 3 * D_SC
DECAY_LANES = 2 * N_SPLIT


def _split_bf16(x):
    pieces = []
    r = x
    for _ in range(N_SPLIT):
        p = r.astype(jnp.bfloat16)
        pieces.append(p)
        r = r - p.astype(jnp.float32)
    return pieces


def _rms(x, g):
    return x * lax.rsqrt(jnp.mean(x * x, axis=-1, keepdims=True) + EPS) * g


def _tile_rows(src_ref, meta_ref, first):
    h = src_ref[0]
    if first:
        h = jnp.where(pl.program_id(1) == 0, meta_ref[...], h)
    return h


def _pad_rows_below(shape):
    limit = jnp.where(pl.program_id(1) == 0, PAD_FRONT, 0)
    return lax.broadcasted_iota(jnp.int32, shape, 0) < limit


def _proj_kernel(src_ref, meta_ref, g_ref, w_ref, bf_ref, tri_ref, e_ref, qkb_ref, lanes_ref,
                 qp_ref, kp_ref, vt_ref, mix_ref, carry_ref, *, first):
    @pl.when(pl.program_id(1) == 0)
    def _():
        carry_ref[...] = jnp.zeros_like(carry_ref)

    hn = _rms(_tile_rows(src_ref, meta_ref, first), g_ref[...]).astype(jnp.bfloat16)

    fq = jnp.dot(hn, w_ref[:, C_F:C_K], preferred_element_type=jnp.float32)
    z = fq[:, :LANES] + bf_ref[...]
    log_f = jnp.minimum(z, 0.0) - jnp.log1p(jnp.exp(-jnp.abs(z)))
    sums = jnp.dot(tri_ref[...], jnp.concatenate(_split_bf16(log_f), axis=-1),
                   preferred_element_type=jnp.float32)
    c = carry_ref[0:1, :]
    for piece in range(N_SPLIT):
        c = c + sums[:, piece * LANES:(piece + 1) * LANES]
    carry_ref[0:1, :] = c[ROW_TILE - 1:ROW_TILE, :]

    c_pieces = jnp.concatenate(_split_bf16(c * LOG2E), axis=-1)
    decay = jnp.dot(c_pieces, e_ref[...], preferred_element_type=jnp.float32) + qkb_ref[...]
    decay_q, decay_k = decay[:, :LANES], decay[:, LANES:]
    decay_k = decay_k + jnp.where(_pad_rows_below((ROW_TILE, LANES)), lanes_ref[HEADS:HEADS + 1, :], 0.0)
    q = fq[:, LANES:] * LOG2E
    k = jnp.dot(hn, w_ref[:, C_K:C_V], preferred_element_type=jnp.float32)
    low_half = lax.broadcasted_iota(jnp.int32, (ROW_TILE, LANES), 1) < HEAD_DIM
    for hd in range(HEADS):
        pair_cols = slice((hd // 2) * LANES, (hd // 2 + 1) * LANES)
        q_h, k_h = q[:, pair_cols], k[:, pair_cols]
        if hd % 2:
            q_h, k_h = pltpu.roll(q_h, HEAD_DIM, axis=1), pltpu.roll(k_h, HEAD_DIM, axis=1)
        qp_ref[0, hd] = jnp.where(low_half, q_h, decay_q).astype(jnp.bfloat16)
        kp_ref[0, hd] = jnp.where(low_half, k_h, decay_k * lanes_ref[hd:hd + 1, :]).astype(jnp.bfloat16)

    v = jnp.dot(hn, w_ref[:, C_V:C_REST], preferred_element_type=jnp.float32)
    vt = v.T.astype(jnp.bfloat16)
    ones_row = (lax.broadcasted_iota(jnp.int32, (VT_ROWS - HEAD_DIM, ROW_TILE), 0) == 0).astype(jnp.bfloat16)
    for hd in range(HEADS):
        vt_ref[0, hd, 0:HEAD_DIM, :] = vt[hd * HEAD_DIM:(hd + 1) * HEAD_DIM, :]
        vt_ref[0, hd, HEAD_DIM:, :] = ones_row

    r = jnp.dot(hn, w_ref[:, C_REST:N_PACK], preferred_element_type=jnp.float32)
    conf_a = r[:, 0:D_CONF]
    conf_g = r[:, D_CONF:2 * D_CONF]
    sc_b = r[:, 2 * D_CONF:2 * D_CONF + D_SC]
    sc_c = r[:, 2 * D_CONF + D_SC:2 * D_CONF + 2 * D_SC]
    sc_u = r[:, 2 * D_CONF + 2 * D_SC:]
    mix_ref[0, :, 0:D_CONF] = conf_a * jax.nn.sigmoid(conf_g)
    mix_ref[0, :, D_CONF:D_CONF + D_SC] = sc_b
    mix_ref[0, :, D_CONF + D_SC:] = sc_c * sc_u


def _src_spec(first):
    if first:
        return pl.BlockSpec((1, ROW_TILE, D_MODEL), lambda b, i: (b, jnp.maximum(i - 1, 0), 0))
    return pl.BlockSpec((1, ROW_TILE, D_MODEL), lambda b, i: (b, i, 0))


def _layer_spec(shape, layer, **kwargs):
    return pl.BlockSpec((None,) + shape, lambda b, i: (layer,) + (0,) * len(shape), **kwargs)


def _proj_call(src, meta_tile, g, w_pack, bf_row, tri, e_mat, qk_bias, lane_rows, *, layer, first, l_pad):
    bsz = src.shape[0]
    nt = l_pad // ROW_TILE
    const = lambda b, i: (0, 0)
    return pl.pallas_call(
        functools.partial(_proj_kernel, first=first),
        grid=(bsz, nt),
        in_specs=[
            _src_spec(first),
            pl.BlockSpec((ROW_TILE, D_MODEL), const),
            _layer_spec((1, D_MODEL), layer),
            _layer_spec((D_MODEL, N_PACK), layer),
            _layer_spec((1, LANES), layer),
            pl.BlockSpec((ROW_TILE, ROW_TILE), const),
            pl.BlockSpec((N_SPLIT * LANES, 2 * LANES), const),
            pl.BlockSpec((1, 2 * LANES), const),
            pl.BlockSpec((2 * HEADS, LANES), const),
        ],
        out_specs=[
            pl.BlockSpec((1, HEADS, ROW_TILE, LANES), lambda b, i: (b, 0, i, 0)),
            pl.BlockSpec((1, HEADS, ROW_TILE, LANES), lambda b, i: (b, 0, i, 0)),
            pl.BlockSpec((1, HEADS, VT_ROWS, ROW_TILE), lambda b, i: (b, 0, 0, i)),
            pl.BlockSpec((1, ROW_TILE, D_CONF + 2 * D_SC), lambda b, i: (b, i, 0)),
        ],
        out_shape=[
            jax.ShapeDtypeStruct((bsz, HEADS, l_pad, LANES), jnp.bfloat16),
            jax.ShapeDtypeStruct((bsz, HEADS, l_pad, LANES), jnp.bfloat16),
            jax.ShapeDtypeStruct((bsz, HEADS, VT_ROWS, l_pad), jnp.bfloat16),
            jax.ShapeDtypeStruct((bsz, l_pad, D_CONF + 2 * D_SC), jnp.float32),
        ],
        scratch_shapes=[pltpu.VMEM((8, LANES), jnp.float32)],
        compiler_params=pltpu.CompilerParams(
            dimension_semantics=("arbitrary", "arbitrary"), vmem_limit_bytes=VMEM_LIMIT),
        name="proj_first" if first else "proj",
    )(src, meta_tile, g, w_pack, bf_row, tri, e_mat, qk_bias, lane_rows)


def _attn_kernel(qp_ref, kp_ref, vt_ref, o_ref, sa_ref, sb_ref):
    qi = pl.program_id(2)
    heads = range(HEADS_PER_STEP)
    qs = [qp_ref[0, hd] for hd in heads]

    def scores(j, s_ref, diag=None):
        k0 = pl.multiple_of(j * BK, BK)
        q0 = 0 if diag is None else diag * BK
        for hd in heads:
            s_ref[hd, :, q0:] = lax.dot_general(kp_ref[0, hd, pl.ds(k0, BK), :], qs[hd][q0:],
                                                (((1,), (1,)), ((), ())),
                                                preferred_element_type=jnp.float32)

    def consume(j, s_ref, carries, diag):
        k0 = pl.multiple_of(j * BK, BK)
        q0 = 0 if diag is None else diag * BK
        out = []
        for hd in heads:
            m, acc = (c[:, q0:] for c in carries[hd])
            s = s_ref[hd, :, q0:]
            if diag is not None:
                kpos = lax.broadcasted_iota(jnp.int32, s.shape, 0) + diag * BK
                qpos = lax.broadcasted_iota(jnp.int32, s.shape, 1) + q0
                s = jnp.where(kpos <= qpos, s, -jnp.inf)
            m_new = jnp.maximum(m, jnp.max(s, axis=0, keepdims=True))
            p = jnp.exp2(s - m_new)
            vblk = vt_ref[0, hd, :, pl.ds(k0, BK)]
            acc = jnp.exp2(m - m_new) * acc + jnp.dot(vblk, p.astype(jnp.bfloat16),
                                                      preferred_element_type=jnp.float32)
            new = (m_new, acc)
            if q0:
                new = tuple(jnp.concatenate([c[:, :q0], n], axis=1) for c, n in zip(carries[hd], new))
            out.append(new)
        return tuple(out)

    def finish(carries):
        for hd, (m, acc) in enumerate(carries):
            o_ref[0, hd * HEAD_DIM:(hd + 1) * HEAD_DIM, :] = (acc[0:HEAD_DIM] / acc[HEAD_DIM:HEAD_DIM + 1]
                                                             ).astype(o_ref.dtype)

    init = ((jnp.full((1, BQ), -jnp.inf, jnp.float32), jnp.zeros((VT_ROWS, BQ), jnp.float32)),) * HEADS_PER_STEP

    @pl.when(qi == 0)
    def _():
        scores(0, sa_ref, 0)
        scores(1, sb_ref, 1)
        finish(consume(1, sb_ref, consume(0, sa_ref, init, 0), 1))

    def pair(t, carries):
        j = 2 * t + 1
        scores(j + 1, sb_ref)
        carries = consume(j, sa_ref, carries, None)
        scores(j + 2, sa_ref)
        return consume(j + 1, sb_ref, carries, None)

    def pairs(t0, n, carries):
        for u in range(n):
            carries = pair(t0 + u, carries)
        return carries

    def closing(carries):
        last = 2 * qi - 1
        scores(last + 1, sb_ref, 0)
        carries = consume(last, sa_ref, carries, None)
        scores(last + 2, sa_ref, 1)
        carries = consume(last + 1, sb_ref, carries, 0)
        finish(consume(last + 2, sa_ref, carries, 1))

    @pl.when(qi == 1)
    def _():
        scores(1, sa_ref)
        closing(init)

    @pl.when(qi > 1)
    def _():
        scores(1, sa_ref)
        n_pairs = qi - 2
        trips = n_pairs // PAIRS_PER_TRIP
        carries = lax.fori_loop(0, trips, lambda t, cr: pairs(PAIRS_PER_TRIP * t, PAIRS_PER_TRIP, cr), init)
        done = trips * PAIRS_PER_TRIP
        group = PAIRS_PER_TRIP // 2
        while group:
            take = (n_pairs - done) & group
            carries = lax.cond(take != 0, functools.partial(pairs, done, group), lambda cr: cr, carries)
            done, group = done + take, group // 2
        closing(pair(n_pairs, carries))


def _attn_call(qp, kp, vt):
    bsz, heads, l_pad, _ = qp.shape
    nq = l_pad // BQ
    hps = HEADS_PER_STEP
    return pl.pallas_call(
        _attn_kernel,
        grid=(bsz, heads // hps, nq),
        in_specs=[
            pl.BlockSpec((1, hps, BQ, LANES), lambda b, h, i: (b, h, i, 0)),
            pl.BlockSpec((1, hps, l_pad, LANES), lambda b, h, i: (b, h, 0, 0)),
            pl.BlockSpec((1, hps, VT_ROWS, l_pad), lambda b, h, i: (b, h, 0, 0)),
        ],
        out_specs=pl.BlockSpec((1, hps * HEAD_DIM, BQ), lambda b, h, i: (b, h, i)),
        out_shape=jax.ShapeDtypeStruct((bsz, heads * HEAD_DIM, l_pad), jnp.bfloat16),
        scratch_shapes=[pltpu.VMEM((hps, BK, BQ), jnp.float32), pltpu.VMEM((hps, BK, BQ), jnp.float32)],
        compiler_params=pltpu.CompilerParams(
            dimension_semantics=("arbitrary", "arbitrary", "arbitrary"), vmem_limit_bytes=VMEM_LIMIT),
        name="attn",
    )(qp, kp, vt)


def _causal_taps(ext_ref, w_ref, n_taps, k_lo, k_hi, shift_ref=None):
    out = None
    for k in range(k_lo, k_hi):
        start = HALO - (n_taps - 1) + k
        if shift_ref is None or start % SUBLANES == 0:
            window = ext_ref[pl.ds(start, ROW_TILE), :]
        else:
            window = shift_ref[start % SUBLANES - 1, pl.ds(start - start % SUBLANES, ROW_TILE), :]
        term = window * w_ref[k:k + 1, :]
        out = term if out is None else out + term
    return out


def _zero_of(x):
    bits = pltpu.bitcast(x, jnp.uint32)
    return lax.shift_right_logical(lax.shift_right_logical(bits, jnp.uint32(16)), jnp.uint32(16)).astype(jnp.float32)


def _conv_parts(mixt_ref, halo, wdw_ref, bdw_ref, lng_ref, lnb_ref, wsc_ref, ext_a, ext_c, shift_a, y_ref, sc_ref):
    tap_parts = CONV_PARTS - 2
    bounds = [round(i * CONF_K / tap_parts) for i in range(tap_parts + 1)]
    state = {}

    def first_part(edge):
        ext_a[0:HALO, :] = halo[:, 0:D_CONF] + edge
        ext_a[HALO:, :] = mixt_ref[0, :, 0:D_CONF]
        ext_c[0:HALO, :] = halo[:, D_CONF + D_SC:] + edge
        ext_c[HALO:, :] = mixt_ref[0, :, D_CONF + D_SC:]
        for r in range(1, SUBLANES):
            shift_a[r - 1] = ext_a[pl.ds(r, SHIFT_ROWS), :]
        state["dw"] = bdw_ref[...]

    def middle_part(part, edge):
        state["dw"] = state["dw"] + edge + _causal_taps(ext_a, wdw_ref, CONF_K, bounds[part - 1], bounds[part],
                                                        shift_a)

    def last_part(edge):
        dw = state["dw"] + edge
        mu = jnp.mean(dw, axis=-1, keepdims=True)
        xc = dw - mu
        y = xc * lax.rsqrt(jnp.mean(xc * xc, axis=-1, keepdims=True) + EPS) * lng_ref[...] + lnb_ref[...]
        y = y * jax.nn.sigmoid(y)
        sc = mixt_ref[0, :, D_CONF:D_CONF + D_SC] * _causal_taps(ext_c, wsc_ref, SC_K, 0, SC_K)
        y_ref[...] = y.astype(y_ref.dtype)
        sc_ref[...] = sc.astype(sc_ref.dtype)
        return _zero_of(jnp.max(y, axis=0, keepdims=True) + jnp.max(sc, axis=0, keepdims=True))

    middle = [functools.partial(middle_part, part) for part in range(1, CONV_PARTS - 1)]
    return [first_part] + middle + [last_part]


def _mix_kernel(src_ref, meta_ref, ot_ref, mix_ref, mixn_ref, wdw_ref, bdw_ref, lng_ref, lnb_ref, wpw_ref,
                bpw_ref, wsc_ref, wout_ref, ng_ref, w1_ref, w2_ref, fg_ref, out_ref,
                ext_a, ext_c, shift_a, y_ref, sc_ref, *, first, final):
    conv_parts = functools.partial(_conv_parts, wdw_ref=wdw_ref, bdw_ref=bdw_ref, lng_ref=lng_ref, lnb_ref=lnb_ref,
                                   wsc_ref=wsc_ref, ext_a=ext_a, ext_c=ext_c, shift_a=shift_a,
                                   y_ref=y_ref, sc_ref=sc_ref)

    @pl.when(pl.program_id(1) == 0)
    def _():
        for part in conv_parts(mix_ref, jnp.zeros((HALO, D_CONF + 2 * D_SC), jnp.float32)):
            part(0.0)

    conf = jnp.dot(y_ref[...], wpw_ref[...], preferred_element_type=jnp.float32) + bpw_ref[...]
    sc = sc_ref[...]

    attn = ot_ref[0].T
    mixed = (jnp.dot(attn, wout_ref[0:D_ATTN, :], preferred_element_type=jnp.float32)
             + jnp.dot(conf.astype(jnp.bfloat16), wout_ref[D_ATTN:D_ATTN + D_CONF, :],
                       preferred_element_type=jnp.float32)
             + jnp.dot(sc, wout_ref[D_ATTN + D_CONF:, :], preferred_element_type=jnp.float32))
    h1 = _tile_rows(src_ref, meta_ref, first) + mixed

    next_parts = conv_parts(mixn_ref, mix_ref[0, ROW_TILE - HALO:, :])
    hn = _rms(h1, ng_ref[...]).astype(jnp.bfloat16)
    acc = h1
    for chunk, c0 in enumerate(range(0, D_FF, FF_CHUNK)):
        if chunk == CONV_PARTS:
            hn = hn + jnp.concatenate([conv_zero] * (D_MODEL // D_CONF), axis=1).astype(jnp.bfloat16)
        a = jnp.dot(hn, w1_ref[:, c0:c0 + FF_CHUNK], preferred_element_type=jnp.float32)
        if chunk < CONV_PARTS:
            conv_zero = next_parts[chunk](_zero_of(a[0:1, 0:D_CONF]))
        a = jnp.maximum(a, 0.0)
        acc = acc + jnp.dot((a * a).astype(jnp.bfloat16), w2_ref[c0:c0 + FF_CHUNK, :],
                            preferred_element_type=jnp.float32)
    if final:
        out_ref[0] = _rms(acc, fg_ref[...])
    else:
        out_ref[0] = jnp.where(_pad_rows_below(acc.shape), 0.0, acc)


def _mix_call(src, meta_tile, ot, mixin, wdw, bdw, lng, lnb, wpw, bpw, wsc, wout, ng, w1, w2, fg,
              *, layer, first, final, l_pad):
    bsz = src.shape[0]
    nt = l_pad // ROW_TILE
    const = lambda b, i: (0, 0)
    once = pl.Buffered(1)
    wide = D_CONF + 2 * D_SC
    if final:
        out_spec = pl.BlockSpec((1, ROW_TILE, D_MODEL), lambda b, i: (b, jnp.maximum(i - 1, 0), 0))
        out_shape = jax.ShapeDtypeStruct((bsz, l_pad - ROW_TILE, D_MODEL), jnp.float32)
    else:
        out_spec = pl.BlockSpec((1, ROW_TILE, D_MODEL), lambda b, i: (b, i, 0))
        out_shape = jax.ShapeDtypeStruct((bsz, l_pad, D_MODEL), jnp.float32)
    return pl.pallas_call(
        functools.partial(_mix_kernel, first=first, final=final),
        grid=(bsz, nt),
        in_specs=[
            _src_spec(first),
            pl.BlockSpec((ROW_TILE, D_MODEL), const),
            pl.BlockSpec((1, D_ATTN, ROW_TILE), lambda b, i: (b, 0, i)),
            pl.BlockSpec((1, ROW_TILE, wide), lambda b, i: (b, i, 0)),
            pl.BlockSpec((1, ROW_TILE, wide), lambda b, i: (b, jnp.minimum(i + 1, nt - 1), 0)),
            _layer_spec((HALO, D_CONF), layer),
            _layer_spec((1, D_CONF), layer),
            _layer_spec((1, D_CONF), layer),
            _layer_spec((1, D_CONF), layer),
            _layer_spec((D_CONF, D_CONF), layer),
            _layer_spec((1, D_CONF), layer),
            _layer_spec((SUBLANES, D_SC), layer),
            _layer_spec((D_MODEL, D_MODEL), layer, pipeline_mode=once),
            _layer_spec((1, D_MODEL), layer),
            _layer_spec((D_MODEL, D_FF), layer, pipeline_mode=once),
            _layer_spec((D_FF, D_MODEL), layer, pipeline_mode=once),
            pl.BlockSpec((1, D_MODEL), const),
        ],
        out_specs=out_spec,
        out_shape=out_shape,
        scratch_shapes=[pltpu.VMEM((HALO + ROW_TILE, D_CONF), jnp.float32),
                        pltpu.VMEM((HALO + ROW_TILE, D_SC), jnp.float32),
                        pltpu.VMEM((SUBLANES - 1, SHIFT_ROWS, D_CONF), jnp.float32),
                        pltpu.VMEM((ROW_TILE, D_CONF), jnp.bfloat16),
                        pltpu.VMEM((ROW_TILE, D_SC), jnp.bfloat16)],
        compiler_params=pltpu.CompilerParams(
            dimension_semantics=("arbitrary", "arbitrary"), vmem_limit_bytes=VMEM_LIMIT),
        name="mix_final" if final else "mix",
    )(src, meta_tile, ot, mixin, mixin, wdw, bdw, lng, lnb, wpw, bpw, wsc, wout, ng, w1, w2, fg)


def _decay_selectors():
    assert HEAD_DIM + HEADS * DECAY_LANES <= LANES
    e = np.zeros((N_SPLIT * LANES, 2 * LANES), np.float32)
    bias = np.zeros((1, 2 * LANES), np.float32)
    lane_rows = np.zeros((2 * HEADS, LANES), np.float32)
    for hd in range(HEADS):
        lane0 = HEAD_DIM + hd * DECAY_LANES
        lane_rows[hd, lane0:lane0 + DECAY_LANES] = 1.0
        lane_rows[HEADS, lane0 + N_SPLIT] = PAD_KEY_BIAS
        for j in range(N_SPLIT):
            e[j * LANES + hd, lane0 + j] = 1.0
            bias[0, lane0 + N_SPLIT + j] = 1.0
            bias[0, LANES + lane0 + j] = 1.0
            e[j * LANES + hd, LANES + lane0 + N_SPLIT + j] = -1.0
    return jnp.asarray(e, jnp.bfloat16), jnp.asarray(bias, jnp.float32), jnp.asarray(lane_rows, jnp.float32)


def _pack_w_in(w):
    q = w[..., 0:D_ATTN] * (HEAD_DIM ** -0.5)
    k = w[..., D_ATTN:2 * D_ATTN]
    v = w[..., 2 * D_ATTN:3 * D_ATTN]
    f = jnp.pad(w[..., 3 * D_ATTN:3 * D_ATTN + HEADS], ((0, 0), (0, 0), (0, LANES - HEADS)))
    rest = w[..., 3 * D_ATTN + HEADS:]
    return jnp.concatenate([f, q, k, v, rest], -1).astype(jnp.bfloat16)


def kernel(x, meta_tokens, mix_norm_g, w_in, b_forget, w_conf_dw, b_conf_dw, conf_ln_g, conf_ln_b,
           w_conf_pw, b_conf_pw, w_sc_conv, w_out, mlp_norm_g, w_mlp1, w_mlp2, final_norm_g):
    bsz, seq, d = x.shape
    depth = w_in.shape[0]
    assert seq % ROW_TILE == 0 and meta_tokens.shape[0] == N_META and PAD_FRONT >= BK and BQ == 2 * BK
    l_pad = ROW_TILE + seq

    meta_tile = jnp.concatenate([jnp.zeros((PAD_FRONT, d), x.dtype), meta_tokens.astype(x.dtype)], axis=0)
    tri = jnp.asarray(np.tril(np.ones((ROW_TILE, ROW_TILE), np.float32)), jnp.bfloat16)
    e_mat, qk_bias, lane_rows = _decay_selectors()

    rows = lambda a: a[:, None, :].astype(jnp.float32)
    bf16 = lambda a: a.astype(jnp.bfloat16)
    w_pack, bf_rows = _pack_w_in(w_in), jnp.pad(rows(b_forget), ((0, 0), (0, 0), (0, LANES - HEADS)))
    wdw = jnp.pad(w_conf_dw, ((0, 0), (0, HALO - CONF_K), (0, 0)))
    wsc = jnp.pad(w_sc_conv, ((0, 0), (0, SUBLANES - SC_K), (0, 0)))
    mix_params = (wdw, rows(b_conf_dw), rows(conf_ln_g), rows(conf_ln_b), bf16(w_conf_pw), rows(b_conf_pw), wsc,
                  bf16(w_out), rows(mlp_norm_g), bf16(w_mlp1), bf16(w_mlp2),
                  final_norm_g.reshape(1, -1).astype(jnp.float32))
    g_rows = rows(mix_norm_g)

    h = x
    for l in range(depth):
        first, final = l == 0, l == depth - 1
        qp, kp, vt, mixin = _proj_call(h, meta_tile, g_rows, w_pack, bf_rows, tri, e_mat, qk_bias, lane_rows,
                                       layer=l, first=first, l_pad=l_pad)
        ot = _attn_call(qp, kp, vt)
        h = _mix_call(h, meta_tile, ot, mixin, *mix_params, layer=l, first=first, final=final, l_pad=l_pad)
    return h
```

```python
import functools

import jax
import jax.numpy as jnp
import numpy as np
from jax import lax
from jax.experimental import pallas as pl
from jax.experimental.pallas import tpu as pltpu

D_MODEL = 1024
N_META = 16
HEADS = 8
HEAD_DIM = 64
D_ATTN = HEADS * HEAD_DIM
D_CONF = 256
D_SC = 256
CONF_K = 31
SC_K = 3
D_FF = 4 * D_MODEL
EPS = 1e-6

LANES = 128
SUBLANES = 8
ROW_TILE = 512
PAD_FRONT = ROW_TILE - N_META
BQ = ROW_TILE
BK = 256
HEADS_PER_STEP = 2
PAIRS_PER_TRIP = 8
LOG2E = 1.4426950408889634
VT_ROWS = HEAD_DIM + 16
PAD_KEY_BIAS = -2.0 ** 100
HALO = 32
SHIFT_ROWS = ROW_TILE + HALO - SUBLANES
FF_CHUNK = 512
CONV_PARTS = 7
N_SPLIT = 3
VMEM_LIMIT = 56 * 1024 * 1024

C_F = 0
C_Q = C_F + LANES
C_K = C_Q + D_ATTN
C_V = C_K + D_ATTN
C_REST = C_V + D_ATTN
N_PACK = C_REST + 2 * D_CONF + 3 * D_SC
DECAY_LANES = 2 * N_SPLIT


def _split_bf16(x):
    pieces = []
    r = x
    for _ in range(N_SPLIT):
        p = r.astype(jnp.bfloat16)
        pieces.append(p)
        r = r - p.astype(jnp.float32)
    return pieces


def _rms(x, g):
    return x * lax.rsqrt(jnp.mean(x * x, axis=-1, keepdims=True) + EPS) * g


def _dot_t(x, w_t):
    return lax.dot_general(x, w_t, (((1,), (1,)), ((), ())), preferred_element_type=jnp.float32)


def _tile_rows(src_ref, meta_ref, first):
    h = src_ref[0]
    if first:
        h = jnp.where(pl.program_id(1) == 0, meta_ref[...], h)
    return h


def _pad_rows_below(shape):
    limit = jnp.where(pl.program_id(1) == 0, PAD_FRONT, 0)
    return lax.broadcasted_iota(jnp.int32, shape, 0) < limit


def _proj_kernel(src_ref, meta_ref, g_ref, w_ref, bf_ref, tri_ref, e_ref, qkb_ref, lanes_ref,
                 qp_ref, kp_ref, vt_ref, mix_ref, carry_ref, *, first):
    @pl.when(pl.program_id(1) == 0)
    def _():
        carry_ref[...] = jnp.zeros_like(carry_ref)

    hn = _rms(_tile_rows(src_ref, meta_ref, first), g_ref[...]).astype(jnp.bfloat16)

    fq = _dot_t(hn, w_ref[C_F:C_K, :])
    z = fq[:, :LANES] + bf_ref[...]
    log_f = jnp.minimum(z, 0.0) - jnp.log1p(jnp.exp(-jnp.abs(z)))
    sums = jnp.dot(tri_ref[...], jnp.concatenate(_split_bf16(log_f), axis=-1),
                   preferred_element_type=jnp.float32)
    c = carry_ref[0:1, :]
    for piece in range(N_SPLIT):
        c = c + sums[:, piece * LANES:(piece + 1) * LANES]
    carry_ref[0:1, :] = c[ROW_TILE - 1:ROW_TILE, :]

    c_pieces = jnp.concatenate(_split_bf16(c * LOG2E), axis=-1)
    decay = jnp.dot(c_pieces, e_ref[...], preferred_element_type=jnp.float32) + qkb_ref[...]
    decay_q, decay_k = decay[:, :LANES], decay[:, LANES:]
    decay_k = decay_k + jnp.where(_pad_rows_below((ROW_TILE, LANES)), lanes_ref[HEADS:HEADS + 1, :], 0.0)
    q = fq[:, LANES:] * LOG2E
    k = _dot_t(hn, w_ref[C_K:C_V, :])
    low_half = lax.broadcasted_iota(jnp.int32, (ROW_TILE, LANES), 1) < HEAD_DIM
    for hd in range(HEADS):
        pair_cols = slice((hd // 2) * LANES, (hd // 2 + 1) * LANES)
        q_h, k_h = q[:, pair_cols], k[:, pair_cols]
        if hd % 2:
            q_h, k_h = pltpu.roll(q_h, HEAD_DIM, axis=1), pltpu.roll(k_h, HEAD_DIM, axis=1)
        qp_ref[0, hd] = jnp.where(low_half, q_h, decay_q).astype(jnp.bfloat16)
        kp_ref[0, hd] = jnp.where(low_half, k_h, decay_k * lanes_ref[hd:hd + 1, :]).astype(jnp.bfloat16)

    v = _dot_t(hn, w_ref[C_V:C_REST, :])
    vt = v.T.astype(jnp.bfloat16)
    ones_row = (lax.broadcasted_iota(jnp.int32, (VT_ROWS - HEAD_DIM, ROW_TILE), 0) == 0).astype(jnp.bfloat16)
    for hd in range(HEADS):
        vt_ref[0, hd, 0:HEAD_DIM, :] = vt[hd * HEAD_DIM:(hd + 1) * HEAD_DIM, :]
        vt_ref[0, hd, HEAD_DIM:, :] = ones_row

    r = _dot_t(hn, w_ref[C_REST:N_PACK, :])
    conf_a = r[:, 0:D_CONF]
    conf_g = r[:, D_CONF:2 * D_CONF]
    sc_b = r[:, 2 * D_CONF:2 * D_CONF + D_SC]
    sc_c = r[:, 2 * D_CONF + D_SC:2 * D_CONF + 2 * D_SC]
    sc_u = r[:, 2 * D_CONF + 2 * D_SC:]
    mix_ref[0, :, 0:D_CONF] = conf_a * jax.nn.sigmoid(conf_g)
    mix_ref[0, :, D_CONF:D_CONF + D_SC] = sc_b
    mix_ref[0, :, D_CONF + D_SC:] = sc_c * sc_u


def _src_spec(first):
    if first:
        return pl.BlockSpec((1, ROW_TILE, D_MODEL), lambda b, i: (b, jnp.maximum(i - 1, 0), 0))
    return pl.BlockSpec((1, ROW_TILE, D_MODEL), lambda b, i: (b, i, 0))


def _layer_spec(shape, layer, **kwargs):
    return pl.BlockSpec((None,) + shape, lambda b, i: (layer,) + (0,) * len(shape), **kwargs)


def _proj_call(src, meta_tile, g, w_pack, bf_row, tri, e_mat, qk_bias, lane_rows, *, layer, first, l_pad):
    bsz = src.shape[0]
    nt = l_pad // ROW_TILE
    const = lambda b, i: (0, 0)
    return pl.pallas_call(
        functools.partial(_proj_kernel, first=first),
        grid=(bsz, nt),
        in_specs=[
            _src_spec(first),
            pl.BlockSpec((ROW_TILE, D_MODEL), const),
            _layer_spec((1, D_MODEL), layer),
            _layer_spec((N_PACK, D_MODEL), layer),
            _layer_spec((1, LANES), layer),
            pl.BlockSpec((ROW_TILE, ROW_TILE), const),
            pl.BlockSpec((N_SPLIT * LANES, 2 * LANES), const),
            pl.BlockSpec((1, 2 * LANES), const),
            pl.BlockSpec((2 * HEADS, LANES), const),
        ],
        out_specs=[
            pl.BlockSpec((1, HEADS, ROW_TILE, LANES), lambda b, i: (b, 0, i, 0)),
            pl.BlockSpec((1, HEADS, ROW_TILE, LANES), lambda b, i: (b, 0, i, 0)),
            pl.BlockSpec((1, HEADS, VT_ROWS, ROW_TILE), lambda b, i: (b, 0, 0, i)),
            pl.BlockSpec((1, ROW_TILE, D_CONF + 2 * D_SC), lambda b, i: (b, i, 0)),
        ],
        out_shape=[
            jax.ShapeDtypeStruct((bsz, HEADS, l_pad, LANES), jnp.bfloat16),
            jax.ShapeDtypeStruct((bsz, HEADS, l_pad, LANES), jnp.bfloat16),
            jax.ShapeDtypeStruct((bsz, HEADS, VT_ROWS, l_pad), jnp.bfloat16),
            jax.ShapeDtypeStruct((bsz, l_pad, D_CONF + 2 * D_SC), jnp.float32),
        ],
        scratch_shapes=[pltpu.VMEM((8, LANES), jnp.float32)],
        compiler_params=pltpu.CompilerParams(
            dimension_semantics=("arbitrary", "arbitrary"), vmem_limit_bytes=VMEM_LIMIT),
        name="proj_first" if first else "proj",
    )(src, meta_tile, g, w_pack, bf_row, tri, e_mat, qk_bias, lane_rows)


def _attn_kernel(qp_ref, kp_ref, vt_ref, o_ref, sa_ref, sb_ref):
    qi = pl.program_id(2)
    heads = range(HEADS_PER_STEP)
    qs = [qp_ref[0, hd] for hd in heads]

    def scores(j, s_ref, diag=None):
        k0 = pl.multiple_of(j * BK, BK)
        q0 = 0 if diag is None else diag * BK
        for hd in heads:
            s_ref[hd, :, q0:] = lax.dot_general(kp_ref[0, hd, pl.ds(k0, BK), :], qs[hd][q0:],
                                                (((1,), (1,)), ((), ())),
                                                preferred_element_type=jnp.float32)

    def consume(j, s_ref, carries, diag):
        k0 = pl.multiple_of(j * BK, BK)
        q0 = 0 if diag is None else diag * BK
        out = []
        for hd in heads:
            m, acc = (c[:, q0:] for c in carries[hd])
            s = s_ref[hd, :, q0:]
            if diag is not None:
                kpos = lax.broadcasted_iota(jnp.int32, s.shape, 0) + diag * BK
                qpos = lax.broadcasted_iota(jnp.int32, s.shape, 1) + q0
                s = jnp.where(kpos <= qpos, s, -jnp.inf)
            m_new = jnp.maximum(m, jnp.max(s, axis=0, keepdims=True))
            p = jnp.exp2(s - m_new)
            vblk = vt_ref[0, hd, :, pl.ds(k0, BK)]
            acc = jnp.exp2(m - m_new) * acc + jnp.dot(vblk, p.astype(jnp.bfloat16),
                                                      preferred_element_type=jnp.float32)
            new = (m_new, acc)
            if q0:
                new = tuple(jnp.concatenate([c[:, :q0], n], axis=1) for c, n in zip(carries[hd], new))
            out.append(new)
        return tuple(out)

    def finish(carries):
        for hd, (m, acc) in enumerate(carries):
            o_ref[0, hd * HEAD_DIM:(hd + 1) * HEAD_DIM, :] = (acc[0:HEAD_DIM] / acc[HEAD_DIM:HEAD_DIM + 1]
                                                             ).astype(o_ref.dtype)

    init = ((jnp.full((1, BQ), -jnp.inf, jnp.float32), jnp.zeros((VT_ROWS, BQ), jnp.float32)),) * HEADS_PER_STEP

    @pl.when(qi == 0)
    def _():
        scores(0, sa_ref, 0)
        scores(1, sb_ref, 1)
        finish(consume(1, sb_ref, consume(0, sa_ref, init, 0), 1))

    @pl.when(qi > 0)
    def _():
        def pair(t, carries):
            j = 2 * t + 1
            scores(j + 1, sb_ref)
            carries = consume(j, sa_ref, carries, None)
            scores(j + 2, sa_ref)
            return consume(j + 1, sb_ref, carries, None)

        def pairs(t0, n, carries):
            for u in range(n):
                carries = pair(t0 + u, carries)
            return carries

        scores(1, sa_ref)
        n_pairs = qi - 1
        trips = n_pairs // PAIRS_PER_TRIP
        carries = lax.fori_loop(0, trips, lambda t, cr: pairs(PAIRS_PER_TRIP * t, PAIRS_PER_TRIP, cr), init)
        done = trips * PAIRS_PER_TRIP
        group = PAIRS_PER_TRIP // 2
        while group:
            take = (n_pairs - done) & group
            carries = lax.cond(take != 0, functools.partial(pairs, done, group), lambda cr: cr, carries)
            done, group = done + take, group // 2
        last = 2 * qi - 1
        scores(last + 1, sb_ref, 0)
        carries = consume(last, sa_ref, carries, None)
        scores(last + 2, sa_ref, 1)
        carries = consume(last + 1, sb_ref, carries, 0)
        finish(consume(last + 2, sa_ref, carries, 1))


def _attn_call(qp, kp, vt):
    bsz, heads, l_pad, _ = qp.shape
    nq = l_pad // BQ
    hps = HEADS_PER_STEP
    return pl.pallas_call(
        _attn_kernel,
        grid=(bsz, heads // hps, nq),
        in_specs=[
            pl.BlockSpec((1, hps, BQ, LANES), lambda b, h, i: (b, h, i, 0)),
            pl.BlockSpec((1, hps, l_pad, LANES), lambda b, h, i: (b, h, 0, 0)),
            pl.BlockSpec((1, hps, VT_ROWS, l_pad), lambda b, h, i: (b, h, 0, 0)),
        ],
        out_specs=pl.BlockSpec((1, hps * HEAD_DIM, BQ), lambda b, h, i: (b, h, i)),
        out_shape=jax.ShapeDtypeStruct((bsz, heads * HEAD_DIM, l_pad), jnp.bfloat16),
        scratch_shapes=[pltpu.VMEM((hps, BK, BQ), jnp.float32), pltpu.VMEM((hps, BK, BQ), jnp.float32)],
        compiler_params=pltpu.CompilerParams(
            dimension_semantics=("arbitrary", "arbitrary", "arbitrary"), vmem_limit_bytes=VMEM_LIMIT),
        name="attn",
    )(qp, kp, vt)


def _causal_taps(ext_ref, w_ref, n_taps, k_lo, k_hi, shift_ref=None):
    out = None
    for k in range(k_lo, k_hi):
        start = HALO - (n_taps - 1) + k
        if shift_ref is None or start % SUBLANES == 0:
            window = ext_ref[pl.ds(start, ROW_TILE), :]
        else:
            window = shift_ref[start % SUBLANES - 1, pl.ds(start - start % SUBLANES, ROW_TILE), :]
        term = window * w_ref[k:k + 1, :]
        out = term if out is None else out + term
    return out


def _zero_of(x):
    bits = pltpu.bitcast(x, jnp.uint32)
    return lax.shift_right_logical(lax.shift_right_logical(bits, jnp.uint32(16)), jnp.uint32(16)).astype(jnp.float32)


def _conv_parts(mixt_ref, halo, wdw_ref, bdw_ref, lng_ref, lnb_ref, wsc_ref, ext_a, ext_c, shift_a, y_ref, sc_ref):
    tap_parts = CONV_PARTS - 2
    bounds = [round(i * CONF_K / tap_parts) for i in range(tap_parts + 1)]
    state = {}

    def first_part(edge):
        ext_a[0:HALO, :] = halo[:, 0:D_CONF] + edge
        ext_a[HALO:, :] = mixt_ref[0, :, 0:D_CONF]
        ext_c[0:HALO, :] = halo[:, D_CONF + D_SC:] + edge
        ext_c[HALO:, :] = mixt_ref[0, :, D_CONF + D_SC:]
        for r in range(1, SUBLANES):
            shift_a[r - 1] = ext_a[pl.ds(r, SHIFT_ROWS), :]
        state["dw"] = bdw_ref[...]

    def middle_part(part, edge):
        state["dw"] = state["dw"] + edge + _causal_taps(ext_a, wdw_ref, CONF_K, bounds[part - 1], bounds[part],
                                                        shift_a)

    def last_part(edge):
        dw = state["dw"] + edge
        mu = jnp.mean(dw, axis=-1, keepdims=True)
        xc = dw - mu
        y = xc * lax.rsqrt(jnp.mean(xc * xc, axis=-1, keepdims=True) + EPS) * lng_ref[...] + lnb_ref[...]
        y = y * jax.nn.sigmoid(y)
        sc = mixt_ref[0, :, D_CONF:D_CONF + D_SC] * _causal_taps(ext_c, wsc_ref, SC_K, 0, SC_K)
        y_ref[...] = y.astype(y_ref.dtype)
        sc_ref[...] = sc.astype(sc_ref.dtype)
        return _zero_of(jnp.max(y, axis=0, keepdims=True) + jnp.max(sc, axis=0, keepdims=True))

    middle = [functools.partial(middle_part, part) for part in range(1, CONV_PARTS - 1)]
    return [first_part] + middle + [last_part]


def _mix_kernel(src_ref, meta_ref, ot_ref, mix_ref, mixn_ref, wdw_ref, bdw_ref, lng_ref, lnb_ref, wpw_ref,
                bpw_ref, wsc_ref, wout_ref, ng_ref, w1_ref, w2_ref, fg_ref, out_ref,
                ext_a, ext_c, shift_a, y_ref, sc_ref, *, first, final):
    conv_parts = functools.partial(_conv_parts, wdw_ref=wdw_ref, bdw_ref=bdw_ref, lng_ref=lng_ref, lnb_ref=lnb_ref,
                                   wsc_ref=wsc_ref, ext_a=ext_a, ext_c=ext_c, shift_a=shift_a,
                                   y_ref=y_ref, sc_ref=sc_ref)

    @pl.when(pl.program_id(1) == 0)
    def _():
        for part in conv_parts(mix_ref, jnp.zeros((HALO, D_CONF + 2 * D_SC), jnp.float32)):
            part(0.0)

    conf = jnp.dot(y_ref[...], wpw_ref[...], preferred_element_type=jnp.float32) + bpw_ref[...]
    sc = sc_ref[...]

    attn = ot_ref[0].T
    mixed = (jnp.dot(attn, wout_ref[0:D_ATTN, :], preferred_element_type=jnp.float32)
             + jnp.dot(conf.astype(jnp.bfloat16), wout_ref[D_ATTN:D_ATTN + D_CONF, :],
                       preferred_element_type=jnp.float32)
             + jnp.dot(sc, wout_ref[D_ATTN + D_CONF:, :], preferred_element_type=jnp.float32))
    h1 = _tile_rows(src_ref, meta_ref, first) + mixed

    next_parts = conv_parts(mixn_ref, mix_ref[0, ROW_TILE - HALO:, :])
    hn = _rms(h1, ng_ref[...]).astype(jnp.bfloat16)
    acc = h1
    for chunk, c0 in enumerate(range(0, D_FF, FF_CHUNK)):
        if chunk == CONV_PARTS:
            hn = hn + jnp.concatenate([conv_zero] * (D_MODEL // D_CONF), axis=1).astype(jnp.bfloat16)
        a = jnp.dot(hn, w1_ref[:, c0:c0 + FF_CHUNK], preferred_element_type=jnp.float32)
        if chunk < CONV_PARTS:
            conv_zero = next_parts[chunk](_zero_of(a[0:1, 0:D_CONF]))
        a = jnp.maximum(a, 0.0)
        acc = acc + jnp.dot((a * a).astype(jnp.bfloat16), w2_ref[c0:c0 + FF_CHUNK, :],
                            preferred_element_type=jnp.float32)
    if final:
        out_ref[0] = _rms(acc, fg_ref[...])
    else:
        out_ref[0] = jnp.where(_pad_rows_below(acc.shape), 0.0, acc)


def _mix_call(src, meta_tile, ot, mixin, wdw, bdw, lng, lnb, wpw, bpw, wsc, wout, ng, w1, w2, fg,
              *, layer, first, final, l_pad):
    bsz = src.shape[0]
    nt = l_pad // ROW_TILE
    const = lambda b, i: (0, 0)
    once = pl.Buffered(1)
    wide = D_CONF + 2 * D_SC
    if final:
        out_spec = pl.BlockSpec((1, ROW_TILE, D_MODEL), lambda b, i: (b, jnp.maximum(i - 1, 0), 0))
        out_shape = jax.ShapeDtypeStruct((bsz, l_pad - ROW_TILE, D_MODEL), jnp.float32)
    else:
        out_spec = pl.BlockSpec((1, ROW_TILE, D_MODEL), lambda b, i: (b, i, 0))
        out_shape = jax.ShapeDtypeStruct((bsz, l_pad, D_MODEL), jnp.float32)
    return pl.pallas_call(
        functools.partial(_mix_kernel, first=first, final=final),
        grid=(bsz, nt),
        in_specs=[
            _src_spec(first),
            pl.BlockSpec((ROW_TILE, D_MODEL), const),
            pl.BlockSpec((1, D_ATTN, ROW_TILE), lambda b, i: (b, 0, i)),
            pl.BlockSpec((1, ROW_TILE, wide), lambda b, i: (b, i, 0)),
            pl.BlockSpec((1, ROW_TILE, wide), lambda b, i: (b, jnp.minimum(i + 1, nt - 1), 0)),
            _layer_spec((HALO, D_CONF), layer),
            _layer_spec((1, D_CONF), layer),
            _layer_spec((1, D_CONF), layer),
            _layer_spec((1, D_CONF), layer),
            _layer_spec((D_CONF, D_CONF), layer),
            _layer_spec((1, D_CONF), layer),
            _layer_spec((SUBLANES, D_SC), layer),
            _layer_spec((D_MODEL, D_MODEL), layer, pipeline_mode=once),
            _layer_spec((1, D_MODEL), layer),
            _layer_spec((D_MODEL, D_FF), layer, pipeline_mode=once),
            _layer_spec((D_FF, D_MODEL), layer, pipeline_mode=once),
            pl.BlockSpec((1, D_MODEL), const),
        ],
        out_specs=out_spec,
        out_shape=out_shape,
        scratch_shapes=[pltpu.VMEM((HALO + ROW_TILE, D_CONF), jnp.float32),
                        pltpu.VMEM((HALO + ROW_TILE, D_SC), jnp.float32),
                        pltpu.VMEM((SUBLANES - 1, SHIFT_ROWS, D_CONF), jnp.float32),
                        pltpu.VMEM((ROW_TILE, D_CONF), jnp.bfloat16),
                        pltpu.VMEM((ROW_TILE, D_SC), jnp.bfloat16)],
        compiler_params=pltpu.CompilerParams(
            dimension_semantics=("arbitrary", "arbitrary"), vmem_limit_bytes=VMEM_LIMIT),
        name="mix_final" if final else "mix",
    )(src, meta_tile, ot, mixin, mixin, wdw, bdw, lng, lnb, wpw, bpw, wsc, wout, ng, w1, w2, fg)


def _decay_selectors():
    assert HEAD_DIM + HEADS * DECAY_LANES <= LANES
    e = np.zeros((N_SPLIT * LANES, 2 * LANES), np.float32)
    bias = np.zeros((1, 2 * LANES), np.float32)
    lane_rows = np.zeros((2 * HEADS, LANES), np.float32)
    for hd in range(HEADS):
        lane0 = HEAD_DIM + hd * DECAY_LANES
        lane_rows[hd, lane0:lane0 + DECAY_LANES] = 1.0
        lane_rows[HEADS, lane0 + N_SPLIT] = PAD_KEY_BIAS
        for j in range(N_SPLIT):
            e[j * LANES + hd, lane0 + j] = 1.0
            bias[0, lane0 + N_SPLIT + j] = 1.0
            bias[0, LANES + lane0 + j] = 1.0
            e[j * LANES + hd, LANES + lane0 + N_SPLIT + j] = -1.0
    return jnp.asarray(e, jnp.bfloat16), jnp.asarray(bias, jnp.float32), jnp.asarray(lane_rows, jnp.float32)


def _pack_w_in(w):
    wt = jnp.swapaxes(w, 1, 2)
    q = wt[:, 0:D_ATTN] * (HEAD_DIM ** -0.5)
    k = wt[:, D_ATTN:2 * D_ATTN]
    v = wt[:, 2 * D_ATTN:3 * D_ATTN]
    f = jnp.pad(wt[:, 3 * D_ATTN:3 * D_ATTN + HEADS], ((0, 0), (0, LANES - HEADS), (0, 0)))
    rest = wt[:, 3 * D_ATTN + HEADS:]
    return jnp.concatenate([f, q, k, v, rest], 1).astype(jnp.bfloat16)


def kernel(x, meta_tokens, mix_norm_g, w_in, b_forget, w_conf_dw, b_conf_dw, conf_ln_g, conf_ln_b,
           w_conf_pw, b_conf_pw, w_sc_conv, w_out, mlp_norm_g, w_mlp1, w_mlp2, final_norm_g):
    bsz, seq, d = x.shape
    depth = w_in.shape[0]
    assert seq % ROW_TILE == 0 and meta_tokens.shape[0] == N_META and PAD_FRONT >= BK and BQ == 2 * BK
    l_pad = ROW_TILE + seq

    meta_tile = jnp.concatenate([jnp.zeros((PAD_FRONT, d), x.dtype), meta_tokens.astype(x.dtype)], axis=0)
    tri = jnp.asarray(np.tril(np.ones((ROW_TILE, ROW_TILE), np.float32)), jnp.bfloat16)
    e_mat, qk_bias, lane_rows = _decay_selectors()

    rows = lambda a: a[:, None, :].astype(jnp.float32)
    bf16 = lambda a: a.astype(jnp.bfloat16)
    w_pack, bf_rows = _pack_w_in(w_in), jnp.pad(rows(b_forget), ((0, 0), (0, 0), (0, LANES - HEADS)))
    wdw = jnp.pad(w_conf_dw, ((0, 0), (0, HALO - CONF_K), (0, 0)))
    wsc = jnp.pad(w_sc_conv, ((0, 0), (0, SUBLANES - SC_K), (0, 0)))
    mix_params = (wdw, rows(b_conf_dw), rows(conf_ln_g), rows(conf_ln_b), bf16(w_conf_pw), rows(b_conf_pw), wsc,
                  bf16(w_out), rows(mlp_norm_g), bf16(w_mlp1), bf16(w_mlp2),
                  final_norm_g.reshape(1, -1).astype(jnp.float32))
    g_rows = rows(mix_norm_g)

    h = x
    for l in range(depth):
        first, final = l == 0, l == depth - 1
        qp, kp, vt, mixin = _proj_call(h, meta_tile, g_rows, w_pack, bf_rows, tri, e_mat, qk_bias, lane_rows,
                                       layer=l, first=first, l_pad=l_pad)
        ot = _attn_call(qp, kp, vt)
        h = _mix_call(h, meta_tile, ot, mixin, *mix_params, layer=l, first=first, final=final, l_pad=l_pad)
    return h
```

```python
import functools

import jax
import jax.numpy as jnp
import numpy as np
from jax import lax
from jax.experimental import pallas as pl
from jax.experimental.pallas import tpu as pltpu

D_MODEL = 1024
N_META = 16
HEADS = 8
HEAD_DIM = 64
D_ATTN = HEADS * HEAD_DIM
D_CONF = 256
D_SC = 256
CONF_K = 31
SC_K = 3
D_FF = 4 * D_MODEL
EPS = 1e-6

LANES = 128
SUBLANES = 8
ROW_TILE = 512
PAD_FRONT = ROW_TILE - N_META
BQ = ROW_TILE
BK = 256
HEADS_PER_STEP = 2
CHAIN_LANES = 256
PAIRS_PER_TRIP = 8
LOG2E = 1.4426950408889634
VT_ROWS = HEAD_DIM + 16
PAD_KEY_BIAS = -2.0 ** 100
HALO = 32
SHIFT_ROWS = ROW_TILE + HALO - SUBLANES
FF_CHUNK = 512
CONV_PARTS = 7
N_SPLIT = 3
VMEM_LIMIT = 56 * 1024 * 1024

C_F = 0
C_Q = C_F + LANES
C_K = C_Q + D_ATTN
C_V = C_K + D_ATTN
C_REST = C_V + D_ATTN
N_PACK = C_REST + 2 * D_CONF + 3 * D_SC
DECAY_LANES = 2 * N_SPLIT


def _split_bf16(x):
    pieces = []
    r = x
    for _ in range(N_SPLIT):
        p = r.astype(jnp.bfloat16)
        pieces.append(p)
        r = r - p.astype(jnp.float32)
    return pieces


def _rms(x, g):
    return x * lax.rsqrt(jnp.mean(x * x, axis=-1, keepdims=True) + EPS) * g


def _tile_rows(src_ref, meta_ref, first):
    h = src_ref[0]
    if first:
        h = jnp.where(pl.program_id(1) == 0, meta_ref[...], h)
    return h


def _pad_rows_below(shape):
    limit = jnp.where(pl.program_id(1) == 0, PAD_FRONT, 0)
    return lax.broadcasted_iota(jnp.int32, shape, 0) < limit


def _proj_kernel(src_ref, meta_ref, g_ref, w_ref, bf_ref, tri_ref, e_ref, qkb_ref, lanes_ref,
                 qp_ref, kp_ref, vt_ref, mix_ref, carry_ref, *, first):
    @pl.when(pl.program_id(1) == 0)
    def _():
        carry_ref[...] = jnp.zeros_like(carry_ref)

    hn = _rms(_tile_rows(src_ref, meta_ref, first), g_ref[...]).astype(jnp.bfloat16)

    fq = jnp.dot(hn, w_ref[:, C_F:C_K], preferred_element_type=jnp.float32)
    z = fq[:, :LANES] + bf_ref[...]
    log_f = jnp.minimum(z, 0.0) - jnp.log1p(jnp.exp(-jnp.abs(z)))
    sums = jnp.dot(tri_ref[...], jnp.concatenate(_split_bf16(log_f), axis=-1),
                   preferred_element_type=jnp.float32)
    c = carry_ref[0:1, :]
    for piece in range(N_SPLIT):
        c = c + sums[:, piece * LANES:(piece + 1) * LANES]
    carry_ref[0:1, :] = c[ROW_TILE - 1:ROW_TILE, :]

    c_pieces = jnp.concatenate(_split_bf16(c * LOG2E), axis=-1)
    decay = jnp.dot(c_pieces, e_ref[...], preferred_element_type=jnp.float32) + qkb_ref[...]
    decay_q, decay_k = decay[:, :LANES], decay[:, LANES:]
    decay_k = decay_k + jnp.where(_pad_rows_below((ROW_TILE, LANES)), lanes_ref[HEADS:HEADS + 1, :], 0.0)
    q = fq[:, LANES:] * LOG2E
    k = jnp.dot(hn, w_ref[:, C_K:C_V], preferred_element_type=jnp.float32)
    low_half = lax.broadcasted_iota(jnp.int32, (ROW_TILE, LANES), 1) < HEAD_DIM
    for hd in range(HEADS):
        pair_cols = slice((hd // 2) * LANES, (hd // 2 + 1) * LANES)
        q_h, k_h = q[:, pair_cols], k[:, pair_cols]
        if hd % 2:
            q_h, k_h = pltpu.roll(q_h, HEAD_DIM, axis=1), pltpu.roll(k_h, HEAD_DIM, axis=1)
        qp_ref[0, hd] = jnp.where(low_half, q_h, decay_q).astype(jnp.bfloat16)
        kp_ref[0, hd] = jnp.where(low_half, k_h, decay_k * lanes_ref[hd:hd + 1, :]).astype(jnp.bfloat16)

    v = jnp.dot(hn, w_ref[:, C_V:C_REST], preferred_element_type=jnp.float32)
    vt = v.T.astype(jnp.bfloat16)
    ones_row = (lax.broadcasted_iota(jnp.int32, (VT_ROWS - HEAD_DIM, ROW_TILE), 0) == 0).astype(jnp.bfloat16)
    for hd in range(HEADS):
        vt_ref[0, hd, 0:HEAD_DIM, :] = vt[hd * HEAD_DIM:(hd + 1) * HEAD_DIM, :]
        vt_ref[0, hd, HEAD_DIM:, :] = ones_row

    r = jnp.dot(hn, w_ref[:, C_REST:N_PACK], preferred_element_type=jnp.float32)
    conf_a = r[:, 0:D_CONF]
    conf_g = r[:, D_CONF:2 * D_CONF]
    sc_b = r[:, 2 * D_CONF:2 * D_CONF + D_SC]
    sc_c = r[:, 2 * D_CONF + D_SC:2 * D_CONF + 2 * D_SC]
    sc_u = r[:, 2 * D_CONF + 2 * D_SC:]
    mix_ref[0, :, 0:D_CONF] = conf_a * jax.nn.sigmoid(conf_g)
    mix_ref[0, :, D_CONF:D_CONF + D_SC] = sc_b
    mix_ref[0, :, D_CONF + D_SC:] = sc_c * sc_u


def _src_spec(first):
    if first:
        return pl.BlockSpec((1, ROW_TILE, D_MODEL), lambda b, i: (b, jnp.maximum(i - 1, 0), 0))
    return pl.BlockSpec((1, ROW_TILE, D_MODEL), lambda b, i: (b, i, 0))


def _layer_spec(shape, layer, **kwargs):
    return pl.BlockSpec((None,) + shape, lambda b, i: (layer,) + (0,) * len(shape), **kwargs)


def _proj_call(src, meta_tile, g, w_pack, bf_row, tri, e_mat, qk_bias, lane_rows, *, layer, first, l_pad):
    bsz = src.shape[0]
    nt = l_pad // ROW_TILE
    const = lambda b, i: (0, 0)
    return pl.pallas_call(
        functools.partial(_proj_kernel, first=first),
        grid=(bsz, nt),
        in_specs=[
            _src_spec(first),
            pl.BlockSpec((ROW_TILE, D_MODEL), const),
            _layer_spec((1, D_MODEL), layer),
            _layer_spec((D_MODEL, N_PACK), layer),
            _layer_spec((1, LANES), layer),
            pl.BlockSpec((ROW_TILE, ROW_TILE), const),
            pl.BlockSpec((N_SPLIT * LANES, 2 * LANES), const),
            pl.BlockSpec((1, 2 * LANES), const),
            pl.BlockSpec((2 * HEADS, LANES), const),
        ],
        out_specs=[
            pl.BlockSpec((1, HEADS, ROW_TILE, LANES), lambda b, i: (b, 0, i, 0)),
            pl.BlockSpec((1, HEADS, ROW_TILE, LANES), lambda b, i: (b, 0, i, 0)),
            pl.BlockSpec((1, HEADS, VT_ROWS, ROW_TILE), lambda b, i: (b, 0, 0, i)),
            pl.BlockSpec((1, ROW_TILE, D_CONF + 2 * D_SC), lambda b, i: (b, i, 0)),
        ],
        out_shape=[
            jax.ShapeDtypeStruct((bsz, HEADS, l_pad, LANES), jnp.bfloat16),
            jax.ShapeDtypeStruct((bsz, HEADS, l_pad, LANES), jnp.bfloat16),
            jax.ShapeDtypeStruct((bsz, HEADS, VT_ROWS, l_pad), jnp.bfloat16),
            jax.ShapeDtypeStruct((bsz, l_pad, D_CONF + 2 * D_SC), jnp.float32),
        ],
        scratch_shapes=[pltpu.VMEM((8, LANES), jnp.float32)],
        compiler_params=pltpu.CompilerParams(
            dimension_semantics=("arbitrary", "arbitrary"), vmem_limit_bytes=VMEM_LIMIT),
        name="proj_first" if first else "proj",
    )(src, meta_tile, g, w_pack, bf_row, tri, e_mat, qk_bias, lane_rows)


def _attn_kernel(qp_ref, kp_ref, vt_ref, o_ref, sa_ref, sb_ref):
    qi = pl.program_id(2)
    heads = range(HEADS_PER_STEP)
    qs = [qp_ref[0, hd] for hd in heads]

    def scores(j, s_ref, diag=None):
        k0 = pl.multiple_of(j * BK, BK)
        q0 = 0 if diag is None else diag * BK
        for hd in heads:
            s_ref[hd, :, q0:] = lax.dot_general(kp_ref[0, hd, pl.ds(k0, BK), :], qs[hd][q0:],
                                                (((1,), (1,)), ((), ())),
                                                preferred_element_type=jnp.float32)

    def consume(j, s_ref, carries, diag):
        k0 = pl.multiple_of(j * BK, BK)
        q0 = 0 if diag is None else diag * BK
        out = []
        for hd in heads:
            m, acc = (c[:, q0:] for c in carries[hd])
            m_parts, p_parts = [], []
            for c0 in range(q0, BQ, CHAIN_LANES):
                s = s_ref[hd, :, c0:c0 + CHAIN_LANES]
                if diag is not None:
                    kpos = lax.broadcasted_iota(jnp.int32, s.shape, 0) + diag * BK
                    qpos = lax.broadcasted_iota(jnp.int32, s.shape, 1) + c0
                    s = jnp.where(kpos <= qpos, s, -jnp.inf)
                m_c = jnp.maximum(m[:, c0 - q0:c0 - q0 + CHAIN_LANES], jnp.max(s, axis=0, keepdims=True))
                m_parts.append(m_c)
                p_parts.append(jnp.exp2(s - m_c).astype(jnp.bfloat16))
            m_new = jnp.concatenate(m_parts, axis=1)
            p = jnp.concatenate(p_parts, axis=1)
            vblk = vt_ref[0, hd, :, pl.ds(k0, BK)]
            acc = jnp.exp2(m - m_new) * acc + jnp.dot(vblk, p, preferred_element_type=jnp.float32)
            new = (m_new, acc)
            if q0:
                new = tuple(jnp.concatenate([c[:, :q0], n], axis=1) for c, n in zip(carries[hd], new))
            out.append(new)
        return tuple(out)

    def finish(carries):
        for hd, (m, acc) in enumerate(carries):
            o_ref[0, hd * HEAD_DIM:(hd + 1) * HEAD_DIM, :] = (acc[0:HEAD_DIM] / acc[HEAD_DIM:HEAD_DIM + 1]
                                                             ).astype(o_ref.dtype)

    init = ((jnp.full((1, BQ), -jnp.inf, jnp.float32), jnp.zeros((VT_ROWS, BQ), jnp.float32)),) * HEADS_PER_STEP

    @pl.when(qi == 0)
    def _():
        scores(0, sa_ref, 0)
        scores(1, sb_ref, 1)
        finish(consume(1, sb_ref, consume(0, sa_ref, init, 0), 1))

    @pl.when(qi > 0)
    def _():
        def pair(t, carries):
            j = 2 * t + 1
            scores(j + 1, sb_ref)
            carries = consume(j, sa_ref, carries, None)
            scores(j + 2, sa_ref)
            return consume(j + 1, sb_ref, carries, None)

        def pairs(t0, n, carries):
            for u in range(n):
                carries = pair(t0 + u, carries)
            return carries

        scores(1, sa_ref)
        n_pairs = qi - 1
        trips = n_pairs // PAIRS_PER_TRIP
        carries = lax.fori_loop(0, trips, lambda t, cr: pairs(PAIRS_PER_TRIP * t, PAIRS_PER_TRIP, cr), init)
        done = trips * PAIRS_PER_TRIP
        group = PAIRS_PER_TRIP // 2
        while group:
            take = (n_pairs - done) & group
            carries = lax.cond(take != 0, functools.partial(pairs, done, group), lambda cr: cr, carries)
            done, group = done + take, group // 2
        last = 2 * qi - 1
        scores(last + 1, sb_ref, 0)
        carries = consume(last, sa_ref, carries, None)
        scores(last + 2, sa_ref, 1)
        carries = consume(last + 1, sb_ref, carries, 0)
        finish(consume(last + 2, sa_ref, carries, 1))


def _attn_call(qp, kp, vt):
    bsz, heads, l_pad, _ = qp.shape
    nq = l_pad // BQ
    hps = HEADS_PER_STEP
    return pl.pallas_call(
        _attn_kernel,
        grid=(bsz, heads // hps, nq),
        in_specs=[
            pl.BlockSpec((1, hps, BQ, LANES), lambda b, h, i: (b, h, i, 0)),
            pl.BlockSpec((1, hps, l_pad, LANES), lambda b, h, i: (b, h, 0, 0)),
            pl.BlockSpec((1, hps, VT_ROWS, l_pad), lambda b, h, i: (b, h, 0, 0)),
        ],
        out_specs=pl.BlockSpec((1, hps * HEAD_DIM, BQ), lambda b, h, i: (b, h, i)),
        out_shape=jax.ShapeDtypeStruct((bsz, heads * HEAD_DIM, l_pad), jnp.bfloat16),
        scratch_shapes=[pltpu.VMEM((hps, BK, BQ), jnp.float32), pltpu.VMEM((hps, BK, BQ), jnp.float32)],
        compiler_params=pltpu.CompilerParams(
            dimension_semantics=("arbitrary", "arbitrary", "arbitrary"), vmem_limit_bytes=VMEM_LIMIT),
        name="attn",
    )(qp, kp, vt)


def _causal_taps(ext_ref, w_ref, n_taps, k_lo, k_hi, shift_ref=None):
    out = None
    for k in range(k_lo, k_hi):
        start = HALO - (n_taps - 1) + k
        if shift_ref is None or start % SUBLANES == 0:
            window = ext_ref[pl.ds(start, ROW_TILE), :]
        else:
            window = shift_ref[start % SUBLANES - 1, pl.ds(start - start % SUBLANES, ROW_TILE), :]
        term = window * w_ref[k:k + 1, :]
        out = term if out is None else out + term
    return out


def _zero_of(x):
    bits = pltpu.bitcast(x, jnp.uint32)
    return lax.shift_right_logical(lax.shift_right_logical(bits, jnp.uint32(16)), jnp.uint32(16)).astype(jnp.float32)


def _conv_parts(mixt_ref, halo, wdw_ref, bdw_ref, lng_ref, lnb_ref, wsc_ref, ext_a, ext_c, shift_a, y_ref, sc_ref):
    tap_parts = CONV_PARTS - 2
    bounds = [round(i * CONF_K / tap_parts) for i in range(tap_parts + 1)]
    state = {}

    def first_part(edge):
        ext_a[0:HALO, :] = halo[:, 0:D_CONF] + edge
        ext_a[HALO:, :] = mixt_ref[0, :, 0:D_CONF]
        ext_c[0:HALO, :] = halo[:, D_CONF + D_SC:] + edge
        ext_c[HALO:, :] = mixt_ref[0, :, D_CONF + D_SC:]
        for r in range(1, SUBLANES):
            shift_a[r - 1] = ext_a[pl.ds(r, SHIFT_ROWS), :]
        state["dw"] = bdw_ref[...]

    def middle_part(part, edge):
        state["dw"] = state["dw"] + edge + _causal_taps(ext_a, wdw_ref, CONF_K, bounds[part - 1], bounds[part],
                                                        shift_a)

    def last_part(edge):
        dw = state["dw"] + edge
        mu = jnp.mean(dw, axis=-1, keepdims=True)
        xc = dw - mu
        y = xc * lax.rsqrt(jnp.mean(xc * xc, axis=-1, keepdims=True) + EPS) * lng_ref[...] + lnb_ref[...]
        y = y * jax.nn.sigmoid(y)
        sc = mixt_ref[0, :, D_CONF:D_CONF + D_SC] * _causal_taps(ext_c, wsc_ref, SC_K, 0, SC_K)
        y_ref[...] = y.astype(y_ref.dtype)
        sc_ref[...] = sc.astype(sc_ref.dtype)
        return _zero_of(jnp.max(y, axis=0, keepdims=True) + jnp.max(sc, axis=0, keepdims=True))

    middle = [functools.partial(middle_part, part) for part in range(1, CONV_PARTS - 1)]
    return [first_part] + middle + [last_part]


def _mix_kernel(src_ref, meta_ref, ot_ref, mix_ref, mixn_ref, wdw_ref, bdw_ref, lng_ref, lnb_ref, wpw_ref,
                bpw_ref, wsc_ref, wout_ref, ng_ref, w1_ref, w2_ref, fg_ref, out_ref,
                ext_a, ext_c, shift_a, y_ref, sc_ref, *, first, final):
    conv_parts = functools.partial(_conv_parts, wdw_ref=wdw_ref, bdw_ref=bdw_ref, lng_ref=lng_ref, lnb_ref=lnb_ref,
                                   wsc_ref=wsc_ref, ext_a=ext_a, ext_c=ext_c, shift_a=shift_a,
                                   y_ref=y_ref, sc_ref=sc_ref)

    @pl.when(pl.program_id(1) == 0)
    def _():
        for part in conv_parts(mix_ref, jnp.zeros((HALO, D_CONF + 2 * D_SC), jnp.float32)):
            part(0.0)

    conf = jnp.dot(y_ref[...], wpw_ref[...], preferred_element_type=jnp.float32) + bpw_ref[...]
    sc = sc_ref[...]

    attn = ot_ref[0].T
    mixed = (jnp.dot(attn, wout_ref[0:D_ATTN, :], preferred_element_type=jnp.float32)
             + jnp.dot(conf.astype(jnp.bfloat16), wout_ref[D_ATTN:D_ATTN + D_CONF, :],
                       preferred_element_type=jnp.float32)
             + jnp.dot(sc, wout_ref[D_ATTN + D_CONF:, :], preferred_element_type=jnp.float32))
    h1 = _tile_rows(src_ref, meta_ref, first) + mixed

    next_parts = conv_parts(mixn_ref, mix_ref[0, ROW_TILE - HALO:, :])
    hn = _rms(h1, ng_ref[...]).astype(jnp.bfloat16)
    acc = h1
    for chunk, c0 in enumerate(range(0, D_FF, FF_CHUNK)):
        if chunk == CONV_PARTS:
            hn = hn + jnp.concatenate([conv_zero] * (D_MODEL // D_CONF), axis=1).astype(jnp.bfloat16)
        a = jnp.dot(hn, w1_ref[:, c0:c0 + FF_CHUNK], preferred_element_type=jnp.float32)
        if chunk < CONV_PARTS:
            conv_zero = next_parts[chunk](_zero_of(a[0:1, 0:D_CONF]))
        a = jnp.maximum(a, 0.0)
        acc = acc + jnp.dot((a * a).astype(jnp.bfloat16), w2_ref[c0:c0 + FF_CHUNK, :],
                            preferred_element_type=jnp.float32)
    if final:
        out_ref[0] = _rms(acc, fg_ref[...])
    else:
        out_ref[0] = jnp.where(_pad_rows_below(acc.shape), 0.0, acc)


def _mix_call(src, meta_tile, ot, mixin, wdw, bdw, lng, lnb, wpw, bpw, wsc, wout, ng, w1, w2, fg,
              *, layer, first, final, l_pad):
    bsz = src.shape[0]
    nt = l_pad // ROW_TILE
    const = lambda b, i: (0, 0)
    once = pl.Buffered(1)
    wide = D_CONF + 2 * D_SC
    if final:
        out_spec = pl.BlockSpec((1, ROW_TILE, D_MODEL), lambda b, i: (b, jnp.maximum(i - 1, 0), 0))
        out_shape = jax.ShapeDtypeStruct((bsz, l_pad - ROW_TILE, D_MODEL), jnp.float32)
    else:
        out_spec = pl.BlockSpec((1, ROW_TILE, D_MODEL), lambda b, i: (b, i, 0))
        out_shape = jax.ShapeDtypeStruct((bsz, l_pad, D_MODEL), jnp.float32)
    return pl.pallas_call(
        functools.partial(_mix_kernel, first=first, final=final),
        grid=(bsz, nt),
        in_specs=[
            _src_spec(first),
            pl.BlockSpec((ROW_TILE, D_MODEL), const),
            pl.BlockSpec((1, D_ATTN, ROW_TILE), lambda b, i: (b, 0, i)),
            pl.BlockSpec((1, ROW_TILE, wide), lambda b, i: (b, i, 0)),
            pl.BlockSpec((1, ROW_TILE, wide), lambda b, i: (b, jnp.minimum(i + 1, nt - 1), 0)),
            _layer_spec((HALO, D_CONF), layer),
            _layer_spec((1, D_CONF), layer),
            _layer_spec((1, D_CONF), layer),
            _layer_spec((1, D_CONF), layer),
            _layer_spec((D_CONF, D_CONF), layer),
            _layer_spec((1, D_CONF), layer),
            _layer_spec((SUBLANES, D_SC), layer),
            _layer_spec((D_MODEL, D_MODEL), layer, pipeline_mode=once),
            _layer_spec((1, D_MODEL), layer),
            _layer_spec((D_MODEL, D_FF), layer, pipeline_mode=once),
            _layer_spec((D_FF, D_MODEL), layer, pipeline_mode=once),
            pl.BlockSpec((1, D_MODEL), const),
        ],
        out_specs=out_spec,
        out_shape=out_shape,
        scratch_shapes=[pltpu.VMEM((HALO + ROW_TILE, D_CONF), jnp.float32),
                        pltpu.VMEM((HALO + ROW_TILE, D_SC), jnp.float32),
                        pltpu.VMEM((SUBLANES - 1, SHIFT_ROWS, D_CONF), jnp.float32),
                        pltpu.VMEM((ROW_TILE, D_CONF), jnp.bfloat16),
                        pltpu.VMEM((ROW_TILE, D_SC), jnp.bfloat16)],
        compiler_params=pltpu.CompilerParams(
            dimension_semantics=("arbitrary", "arbitrary"), vmem_limit_bytes=VMEM_LIMIT),
        name="mix_final" if final else "mix",
    )(src, meta_tile, ot, mixin, mixin, wdw, bdw, lng, lnb, wpw, bpw, wsc, wout, ng, w1, w2, fg)


def _decay_selectors():
    assert HEAD_DIM + HEADS * DECAY_LANES <= LANES
    e = np.zeros((N_SPLIT * LANES, 2 * LANES), np.float32)
    bias = np.zeros((1, 2 * LANES), np.float32)
    lane_rows = np.zeros((2 * HEADS, LANES), np.float32)
    for hd in range(HEADS):
        lane0 = HEAD_DIM + hd * DECAY_LANES
        lane_rows[hd, lane0:lane0 + DECAY_LANES] = 1.0
        lane_rows[HEADS, lane0 + N_SPLIT] = PAD_KEY_BIAS
        for j in range(N_SPLIT):
            e[j * LANES + hd, lane0 + j] = 1.0
            bias[0, lane0 + N_SPLIT + j] = 1.0
            bias[0, LANES + lane0 + j] = 1.0
            e[j * LANES + hd, LANES + lane0 + N_SPLIT + j] = -1.0
    return jnp.asarray(e, jnp.bfloat16), jnp.asarray(bias, jnp.float32), jnp.asarray(lane_rows, jnp.float32)


def _pack_w_in(w):
    q = w[..., 0:D_ATTN] * (HEAD_DIM ** -0.5)
    k = w[..., D_ATTN:2 * D_ATTN]
    v = w[..., 2 * D_ATTN:3 * D_ATTN]
    f = jnp.pad(w[..., 3 * D_ATTN:3 * D_ATTN + HEADS], ((0, 0), (0, 0), (0, LANES - HEADS)))
    rest = w[..., 3 * D_ATTN + HEADS:]
    return jnp.concatenate([f, q, k, v, rest], -1).astype(jnp.bfloat16)


def kernel(x, meta_tokens, mix_norm_g, w_in, b_forget, w_conf_dw, b_conf_dw, conf_ln_g, conf_ln_b,
           w_conf_pw, b_conf_pw, w_sc_conv, w_out, mlp_norm_g, w_mlp1, w_mlp2, final_norm_g):
    bsz, seq, d = x.shape
    depth = w_in.shape[0]
    assert seq % ROW_TILE == 0 and meta_tokens.shape[0] == N_META and PAD_FRONT >= BK and BQ == 2 * BK
    l_pad = ROW_TILE + seq

    meta_tile = jnp.concatenate([jnp.zeros((PAD_FRONT, d), x.dtype), meta_tokens.astype(x.dtype)], axis=0)
    tri = jnp.asarray(np.tril(np.ones((ROW_TILE, ROW_TILE), np.float32)), jnp.bfloat16)
    e_mat, qk_bias, lane_rows = _decay_selectors()

    rows = lambda a: a[:, None, :].astype(jnp.float32)
    bf16 = lambda a: a.astype(jnp.bfloat16)
    w_pack, bf_rows = _pack_w_in(w_in), jnp.pad(rows(b_forget), ((0, 0), (0, 0), (0, LANES - HEADS)))
    wdw = jnp.pad(w_conf_dw, ((0, 0), (0, HALO - CONF_K), (0, 0)))
    wsc = jnp.pad(w_sc_conv, ((0, 0), (0, SUBLANES - SC_K), (0, 0)))
    mix_params = (wdw, rows(b_conf_dw), rows(conf_ln_g), rows(conf_ln_b), bf16(w_conf_pw), rows(b_conf_pw), wsc,
                  bf16(w_out), rows(mlp_norm_g), bf16(w_mlp1), bf16(w_mlp2),
                  final_norm_g.reshape(1, -1).astype(jnp.float32))
    g_rows = rows(mix_norm_g)

    h = x
    for l in range(depth):
        first, final = l == 0, l == depth - 1
        qp, kp, vt, mixin = _proj_call(h, meta_tile, g_rows, w_pack, bf_rows, tri, e_mat, qk_bias, lane_rows,
                                       layer=l, first=first, l_pad=l_pad)
        ot = _attn_call(qp, kp, vt)
        h = _mix_call(h, meta_tile, ot, mixin, *mix_params, layer=l, first=first, final=final, l_pad=l_pad)
    return h
```
